```python
import jax, jax.numpy as jnp
from jax import lax
import numpy as np

D_MODEL = 1024
BATCH = 8
SEQ = 4096
DEPTH = 1
DEC_BATCH = 2
DEC_SEQ = 16384
PAST_LEN = 128

N_Q_HEADS = 8
N_KV_HEADS = 2
HEAD_DIM = 64
WINDOW = 128
ATTN_BLOCK = 128
GLA_HEADS = 4
GLA_HEAD_K = 64
GLA_HEAD_V = 128
GLA_GATE_RANK = 16
GLA_GATE_NORMALIZER = 16.0
GLA_CHUNK = 16
N_EXPERTS = 16
EXPERT_HIDDEN = 1024
CAPACITY_FACTOR = 2
NORM_EPS = 1e-6

ATTN_Q_WIDTH = N_Q_HEADS * HEAD_DIM
ATTN_KV_WIDTH = N_KV_HEADS * HEAD_DIM
GLA_KEY_WIDTH = GLA_HEADS * GLA_HEAD_K
GLA_VAL_WIDTH = GLA_HEADS * GLA_HEAD_V
IN_SPLITS = (ATTN_Q_WIDTH, ATTN_KV_WIDTH, ATTN_KV_WIDTH, GLA_KEY_WIDTH, GLA_KEY_WIDTH,
             GLA_VAL_WIDTH, GLA_VAL_WIDTH, GLA_GATE_RANK, GLA_GATE_RANK, D_MODEL, D_MODEL)
IN_WIDTH = (ATTN_Q_WIDTH + 2 * ATTN_KV_WIDTH + 2 * GLA_KEY_WIDTH + 2 * GLA_VAL_WIDTH
            + 2 * GLA_GATE_RANK + 2 * D_MODEL)

kernel_name = 'hybrid_swa_gla_ec_encoder'


def rms_norm(x, w):
    xf = x.astype(jnp.float32)
    y = xf * lax.rsqrt(jnp.mean(xf * xf, axis=-1, keepdims=True) + NORM_EPS)
    return (y * w.astype(jnp.float32)).astype(x.dtype)


def split_cols(t):
    out, start = [], 0
    for w in IN_SPLITS:
        out.append(t[..., start:start + w])
        start += w
    return out


def alibi_slopes(n):
    return jnp.asarray([2.0 ** (-8.0 * (h + 1) / n) for h in range(n)], dtype=jnp.float32)


def windowed_gqa(q, k, v, sink):
    B, S = q.shape[0], q.shape[1]
    L = ATTN_BLOCK
    nb = S // L
    G = N_Q_HEADS // N_KV_HEADS
    qb = q.reshape(B, nb, L, N_KV_HEADS, G, HEAD_DIM)

    def band(t):
        tp = jnp.pad(t, ((0, 0), (L, L), (0, 0), (0, 0))).reshape(B, nb + 2, L, N_KV_HEADS, HEAD_DIM)
        return jnp.concatenate([tp[:, :-2], tp[:, 1:-1], tp[:, 2:]], axis=2)

    kw, vw = band(k), band(v)
    s = jnp.einsum('bnqhgd,bnkhd->bnhgqk', qb, kw).astype(jnp.float32) * (HEAD_DIM ** -0.5)
    dist = jnp.arange(L)[:, None] - jnp.arange(3 * L)[None, :] + L
    kpos = jnp.arange(nb)[:, None] * L + jnp.arange(3 * L)[None, :] - L
    valid = (jnp.abs(dist)[None] <= WINDOW) & ((kpos >= 0) & (kpos < S))[:, None, :]
    slopes = alibi_slopes(N_Q_HEADS).reshape(N_KV_HEADS, G)
    s = s - slopes[:, :, None, None] * jnp.abs(dist).astype(jnp.float32)
    s = jnp.where(valid[None, :, None, None], s, -jnp.inf)
    sk = sink.astype(jnp.float32).reshape(N_KV_HEADS, G)[:, :, None]
    m = jnp.maximum(s.max(-1), sk)
    p = jnp.exp(s - m[..., None])
    denom = p.sum(-1) + jnp.exp(sk - m)
    o = jnp.einsum('bnhgqk,bnkhd->bnqhgd', (p / denom[..., None]).astype(v.dtype), vw)
    return o.reshape(B, S, ATTN_Q_WIDTH)


def gla_chunked(q, k, v, log_a, strict):
    B, S, H, K = q.shape
    V = v.shape[-1]
    C = GLA_CHUNK
    n = S // C
    q = q.reshape(B, n, C, H, K)
    k = k.reshape(B, n, C, H, K)
    b = jnp.cumsum(log_a.reshape(B, n, C, H, K), axis=2)
    v = v.reshape(B, n, C, H, V)
    t_idx = jnp.arange(C)
    mask = (t_idx[:, None] > t_idx[None, :]) if strict else (t_idx[:, None] >= t_idx[None, :])
    diff = b[:, :, :, None] - b[:, :, None, :]
    decay = jnp.exp(jnp.where(mask[:, :, None, None], diff, -jnp.inf))
    A = jnp.einsum('bnthk,bnshk,bntshk->bnhts', q, k, decay)
    o_intra = jnp.einsum('bnhts,bnshv->bnthv', A, v)
    b_last = b[:, :, -1]
    kv = jnp.einsum('bnchk,bnchv->bnhkv', k * jnp.exp(b_last[:, :, None] - b), v)

    def step(state, inp):
        dec, upd = inp
        return dec[..., None] * state + upd, state

    _, s_prev = lax.scan(step, jnp.zeros((B, H, K, V), jnp.float32),
                         (jnp.exp(b_last).transpose(1, 0, 2, 3), kv.transpose(1, 0, 2, 3, 4)))
    s_prev = s_prev.transpose(1, 0, 2, 3, 4)
    o_inter = jnp.einsum('bnchk,bnhkv->bnchv', q * jnp.exp(b), s_prev)
    return (o_intra + o_inter).reshape(B, S, H, V)


def token_mixer(h, p, l):
    B, S, _ = h.shape
    f32 = jnp.float32
    (aq, ak, av, gq, gk, gv, gr, glf, glb, ga, gg) = split_cols(h @ p['w_in'][l])
    attn = windowed_gqa(aq.reshape(B, S, N_Q_HEADS, HEAD_DIM), ak.reshape(B, S, N_KV_HEADS, HEAD_DIM),
                        av.reshape(B, S, N_KV_HEADS, HEAD_DIM), p['attn_sink'][l])
    gq = gq.reshape(B, S, GLA_HEADS, GLA_HEAD_K).astype(f32) * (GLA_HEAD_K ** -0.5)
    gk = gk.reshape(B, S, GLA_HEADS, GLA_HEAD_K).astype(f32)
    gv = gv.reshape(B, S, GLA_HEADS, GLA_HEAD_V).astype(f32)
    la_f = jax.nn.log_sigmoid((glf @ p['gla_wa2_fwd'][l] + p['gla_ba_fwd'][l]).astype(f32)) / GLA_GATE_NORMALIZER
    la_b = jax.nn.log_sigmoid((glb @ p['gla_wa2_bwd'][l] + p['gla_ba_bwd'][l]).astype(f32)) / GLA_GATE_NORMALIZER
    la_f = la_f.reshape(B, S, GLA_HEADS, GLA_HEAD_K)
    la_b = la_b.reshape(B, S, GLA_HEADS, GLA_HEAD_K)
    o_f = gla_chunked(gq, gk, gv, la_f, False)
    o_b = gla_chunked(gq[:, ::-1], gk[:, ::-1], gv[:, ::-1], la_b[:, ::-1], True)[:, ::-1]
    o_g = rms_norm(o_f + o_b, p['gla_norm'][l]).astype(h.dtype).reshape(B, S, GLA_VAL_WIDTH) * jax.nn.silu(gr)
    merged = (jax.nn.sigmoid(ga) * (attn @ p['w_branch_attn'][l])
              + jax.nn.sigmoid(gg) * (o_g @ p['w_branch_gla'][l]))
    return merged @ p['w_out'][l]


def expert_choice_ffn(h, w_router, w_gate, w_up, w_down):
    B, S, D = h.shape
    T = B * S
    cap = CAPACITY_FACTOR * T // N_EXPERTS
    xf = h.reshape(T, D)
    aff = jax.nn.softmax((xf @ w_router).astype(jnp.float32), axis=-1)
    g, idx = lax.top_k(aff.T, cap)
    xs = xf[idx]
    hid = jax.nn.silu(jnp.einsum('ecd,edf->ecf', xs, w_gate)) * jnp.einsum('ecd,edf->ecf', xs, w_up)
    ys = jnp.einsum('ecf,efd->ecd', hid, w_down) * g[..., None].astype(h.dtype)
    out = jnp.zeros((T, D), h.dtype).at[idx.reshape(-1)].add(ys.reshape(-1, D))
    return out.reshape(B, S, D)


def encoder_layer(x, c, p, l):
    ada = jax.nn.silu(c) @ p['w_ada'][l] + p['b_ada'][l]
    sh1, sc1, gt1, sh2, sc2, gt2 = jnp.split(ada[:, None, :], 6, axis=-1)
    h = rms_norm(x, p['norm_pre_mix'][l]) * (1 + sc1) + sh1
    x = x + gt1 * rms_norm(token_mixer(h, p, l), p['norm_post_mix'][l])
    h = rms_norm(x, p['norm_pre_ffn'][l]) * (1 + sc2) + sh2
    y = expert_choice_ffn(h, p['w_router'][l], p['w_gate_e'][l], p['w_up_e'][l], p['w_down_e'][l])
    return x + gt2 * rms_norm(y, p['norm_post_ffn'][l])


def setup_inputs(seed: int = 0) -> dict:
    key = jax.random.key(seed)
    ks = jax.random.split(key, 26)
    nrm = jax.random.normal
    D = D_MODEL
    return {
        'x_prompt': nrm(ks[0], (BATCH, SEQ, D), jnp.float32),
        'x_sample': nrm(ks[1], (DEC_BATCH, DEC_SEQ, D), jnp.float32),
        'c_prompt': nrm(ks[2], (BATCH, D), jnp.float32),
        'c_sample': nrm(ks[3], (DEC_BATCH, D), jnp.float32),
        'w_ada': nrm(ks[4], (DEPTH, D, 6 * D), jnp.float32) * (0.5 * D ** -0.5),
        'b_ada': nrm(ks[5], (DEPTH, 6 * D), jnp.float32) * 0.02,
        'norm_pre_mix': 1.0 + 0.05 * nrm(ks[6], (DEPTH, D), jnp.float32),
        'norm_post_mix': 1.0 + 0.05 * nrm(ks[7], (DEPTH, D), jnp.float32),
        'w_in': nrm(ks[8], (DEPTH, D, IN_WIDTH), jnp.float32) * D ** -0.5,
        'attn_sink': nrm(ks[9], (DEPTH, N_Q_HEADS), jnp.float32) * 0.5,
        'gla_wa2_fwd': nrm(ks[10], (DEPTH, GLA_GATE_RANK, GLA_KEY_WIDTH), jnp.float32) * GLA_GATE_RANK ** -0.5,
        'gla_ba_fwd': nrm(ks[11], (DEPTH, GLA_KEY_WIDTH), jnp.float32) * 0.1,
        'gla_wa2_bwd': nrm(ks[12], (DEPTH, GLA_GATE_RANK, GLA_KEY_WIDTH), jnp.float32) * GLA_GATE_RANK ** -0.5,
        'gla_ba_bwd': nrm(ks[13], (DEPTH, GLA_KEY_WIDTH), jnp.float32) * 0.1,
        'gla_norm': 1.0 + 0.05 * nrm(ks[14], (DEPTH, GLA_HEAD_V), jnp.float32),
        'w_branch_attn': nrm(ks[15], (DEPTH, ATTN_Q_WIDTH, D), jnp.float32) * ATTN_Q_WIDTH ** -0.5,
        'w_branch_gla': nrm(ks[16], (DEPTH, GLA_VAL_WIDTH, D), jnp.float32) * GLA_VAL_WIDTH ** -0.5,
        'w_out': nrm(ks[17], (DEPTH, D, D), jnp.float32) * D ** -0.5,
        'norm_pre_ffn': 1.0 + 0.05 * nrm(ks[18], (DEPTH, D), jnp.float32),
        'norm_post_ffn': 1.0 + 0.05 * nrm(ks[19], (DEPTH, D), jnp.float32),
        'w_router': nrm(ks[20], (DEPTH, D, N_EXPERTS), jnp.float32) * D ** -0.5,
        'w_gate_e': nrm(ks[21], (DEPTH, N_EXPERTS, D, EXPERT_HIDDEN), jnp.float32) * D ** -0.5,
        'w_up_e': nrm(ks[22], (DEPTH, N_EXPERTS, D, EXPERT_HIDDEN), jnp.float32) * D ** -0.5,
        'w_down_e': nrm(ks[23], (DEPTH, N_EXPERTS, EXPERT_HIDDEN, D), jnp.float32) * EXPERT_HIDDEN ** -0.5,
    }


def reference(x_prompt, x_sample, c_prompt, c_sample, w_ada, b_ada, norm_pre_mix, norm_post_mix, w_in,
              attn_sink, gla_wa2_fwd, gla_ba_fwd, gla_wa2_bwd, gla_ba_bwd, gla_norm, w_branch_attn,
              w_branch_gla, w_out, norm_pre_ffn, norm_post_ffn, w_router, w_gate_e, w_up_e, w_down_e):
    p = dict(w_ada=w_ada, b_ada=b_ada, norm_pre_mix=norm_pre_mix, norm_post_mix=norm_post_mix, w_in=w_in,
             attn_sink=attn_sink, gla_wa2_fwd=gla_wa2_fwd, gla_ba_fwd=gla_ba_fwd, gla_wa2_bwd=gla_wa2_bwd,
             gla_ba_bwd=gla_ba_bwd, gla_norm=gla_norm, w_branch_attn=w_branch_attn, w_branch_gla=w_branch_gla,
             w_out=w_out, norm_pre_ffn=norm_pre_ffn, norm_post_ffn=norm_post_ffn, w_router=w_router,
             w_gate_e=w_gate_e, w_up_e=w_up_e, w_down_e=w_down_e)
    y_prompt = x_prompt
    for l in range(DEPTH):
        y_prompt = encoder_layer(y_prompt, c_prompt, p, l)
    y_sample = x_sample
    for l in range(DEPTH):
        y_sample = encoder_layer(y_sample, c_sample, p, l)
    return (y_prompt, y_sample)
```

```python
import functools

import numpy as np
import jax
import jax.numpy as jnp
from jax import lax
from jax.experimental import pallas as pl
from jax.experimental.pallas import tpu as pltpu

F32, BF16, I32 = jnp.float32, jnp.bfloat16, jnp.int32

D = 1024
N_Q_HEADS, N_KV_HEADS, HEAD_DIM = 8, 2, 64
ATTN_BLOCK = 128
GLA_HEADS, GLA_K, GLA_V = 4, 64, 128
GLA_RANK = 16
GLA_NORMALIZER = 16.0
N_EXPERTS, FF = 16, 1024
CAPACITY_FACTOR = 2
EPS = 1e-6
NEG = -1e30

C_AQ, C_GV, C_GA, C_GG, C_GR, C_KV, C_GQ, C_GK, C_GL = 0, 512, 1024, 2048, 3072, 3584, 3840, 4096, 4352
NP = 4480

LANES = 128
ROW_ALIGN = 16
CHUNK = 64
ROUTE_TILE = 256
XS_W = D + LANES
VMEM_LIMIT = 56 * 1024 * 1024


def _cparams(sem):
    return pltpu.CompilerParams(dimension_semantics=sem, vmem_limit_bytes=VMEM_LIMIT)


def _dot(a, b):
    return jnp.dot(a, b, preferred_element_type=F32)


def _dot_nt(a, b):
    return lax.dot_general(a, b, (((1,), (1,)), ((), ())), preferred_element_type=F32)


def _dot_tn(a, b):
    return lax.dot_general(a, b, (((0,), (0,)), ((), ())), preferred_element_type=F32)


def _split2(x):
    hi = x.astype(BF16)
    lo = (x - hi.astype(F32)).astype(BF16)
    return hi, lo


def _split3(x):
    hi = x.astype(BF16)
    r = x - hi.astype(F32)
    mid = r.astype(BF16)
    lo = (r - mid.astype(F32)).astype(BF16)
    return hi, mid, lo


def _rms(x, w):
    return x * lax.rsqrt(jnp.mean(x * x, axis=-1, keepdims=True) + EPS) * w


def _ada_kernel(c_ref, w_ref, b_ref, o_ref):
    c = c_ref[...]
    s = c * jax.nn.sigmoid(c)
    s_hi, s_lo = _split2(s)
    w_hi, w_lo = _split2(w_ref[...])
    o_ref[...] = _dot(s_hi, w_hi) + _dot(s_lo, w_hi) + _dot(s_hi, w_lo) + b_ref[...]


def _ada(c, w_ada, b_ada):
    rows = c.shape[0]
    tn = 1024
    return pl.pallas_call(
        _ada_kernel,
        grid=(6 * D // tn,),
        in_specs=[pl.BlockSpec((rows, D), lambda j: (0, 0)),
                  pl.BlockSpec((D, tn), lambda j: (0, j)),
                  pl.BlockSpec((1, tn), lambda j: (0, j))],
        out_specs=pl.BlockSpec((rows, tn), lambda j: (0, j)),
        out_shape=jax.ShapeDtypeStruct((rows, 6 * D), F32),
        compiler_params=_cparams(("arbitrary",)),
        name="ada",
    )(c, w_ada, b_ada.reshape(1, 6 * D))


def _inproj_kernel(x_ref, ada_ref, nw_ref, w_ref, o_ref):
    x = x_ref[0]
    h = (_rms(x, nw_ref[...]) * (1.0 + ada_ref[0, 1:2, :]) + ada_ref[0, 0:1, :]).astype(BF16)
    step = 640
    for c0 in range(0, NP, step):
        o_ref[0, :, c0:c0 + step] = _dot(h, w_ref[:, c0:c0 + step]).astype(BF16)


def _inproj(x, ada, norm_w, w_in_p):
    B, S, _ = x.shape
    tm = 512
    return pl.pallas_call(
        _inproj_kernel,
        grid=(B, S // tm),
        in_specs=[pl.BlockSpec((1, tm, D), lambda b, i: (b, i, 0)),
                  pl.BlockSpec((1, 6, D), lambda b, i: (b, 0, 0)),
                  pl.BlockSpec((1, D), lambda b, i: (0, 0)),
                  pl.BlockSpec((D, NP), lambda b, i: (0, 0))],
        out_specs=pl.BlockSpec((1, tm, NP), lambda b, i: (b, i, 0)),
        out_shape=jax.ShapeDtypeStruct((B, S, NP), BF16),
        compiler_params=_cparams(("arbitrary", "arbitrary")),
        name="inproj",
    )(x, ada, norm_w.reshape(1, D), w_in_p)


ATTN_TQ = 512
ATTN_QB = ATTN_TQ // ATTN_BLOCK


def _attn_bias_table():
    L = ATTN_BLOCK
    q = np.arange(L)[:, None]
    k = np.arange(3 * L)[None, :]
    dist = np.abs(q - k + L).astype(np.float32)
    slopes = np.asarray([2.0 ** (-8.0 * (h + 1) / N_Q_HEADS) for h in range(N_Q_HEADS)], np.float32)
    base = -slopes[:, None, None] * dist[None]
    inside = (dist <= L)[None]
    variants = []
    for ok_k in (np.ones_like(k, bool), k >= L, k < 2 * L):
        variants.append(np.where(inside & ok_k[None], base, np.float32(NEG)))
    return jnp.asarray(np.stack(variants), F32)


def _attn_kernel(sink_ref, q_ref, kvm_ref, kvp_ref, kvn_ref, bias_ref, o_ref):
    L = ATTN_BLOCK
    i = pl.program_id(1)
    last = pl.num_programs(1) - 1
    kv = jnp.concatenate([kvp_ref[0], kvm_ref[0], kvn_ref[0]], axis=0)
    k = kv[:, :LANES] * jnp.asarray(HEAD_DIM ** -0.5, BF16)
    v = kv[:, LANES:]
    lo = lax.broadcasted_iota(I32, (1, LANES), 1) < HEAD_DIM
    zero = jnp.zeros_like(k)

    def halves(t):
        sw = jnp.concatenate([t[:, HEAD_DIM:], t[:, :HEAD_DIM]], axis=1)
        return ((jnp.where(lo, t, zero), jnp.where(lo, zero, sw)),
                (jnp.where(lo, sw, zero), jnp.where(lo, zero, t)))

    kh, vh = halves(k), halves(v)
    for j in range(ATTN_QB):
        first = jnp.logical_and(i == 0, j == 0)
        final = jnp.logical_and(i == last, j == ATTN_QB - 1)
        var = jnp.where(first, 1, jnp.where(final, 2, 0))
        r0 = j * L
        for m in range(N_Q_HEADS // 2):
            qp = q_ref[0, r0:r0 + L, m * LANES:(m + 1) * LANES]
            hkv = (2 * m) // (N_Q_HEADS // N_KV_HEADS)
            acc = None
            for par in range(2):
                hq = 2 * m + par
                s = _dot_nt(qp, kh[hkv][par][r0:r0 + 3 * L]) + bias_ref[var, hq]
                snk = sink_ref[hq]
                mx = jnp.maximum(jnp.max(s, axis=-1, keepdims=True), snk)
                p = jnp.exp(s - mx)
                den = jnp.sum(p, axis=-1, keepdims=True) + jnp.exp(snk - mx)
                o = _dot(p.astype(BF16), vh[hkv][par][r0:r0 + 3 * L]) * (1.0 / den)
                acc = o if acc is None else acc + o
            o_ref[0, r0:r0 + L, m * LANES:(m + 1) * LANES] = acc.astype(BF16)


def _attention(proj, sink, bias):
    B, S, _ = proj.shape
    L = ATTN_BLOCK
    nb = S // L
    assert S % ATTN_TQ == 0 and nb >= 2
    kvc = C_KV // 256
    return pl.pallas_call(
        _attn_kernel,
        grid=(B, S // ATTN_TQ),
        in_specs=[pl.BlockSpec(memory_space=pltpu.SMEM),
                  pl.BlockSpec((1, ATTN_TQ, 512), lambda b, i: (b, i, C_AQ // 512)),
                  pl.BlockSpec((1, ATTN_TQ, 256), lambda b, i: (b, i, kvc)),
                  pl.BlockSpec((1, L, 256), lambda b, i: (b, jnp.maximum(i * ATTN_QB - 1, 0), kvc)),
                  pl.BlockSpec((1, L, 256), lambda b, i: (b, jnp.minimum(i * ATTN_QB + ATTN_QB, nb - 1), kvc)),
                  pl.BlockSpec((3, N_Q_HEADS, L, 3 * L), lambda b, i: (0, 0, 0, 0))],
        out_specs=pl.BlockSpec((1, ATTN_TQ, 512), lambda b, i: (b, i, 0)),
        out_shape=jax.ShapeDtypeStruct((B, S, 512), BF16),
        compiler_params=_cparams(("arbitrary", "arbitrary")),
        name="attn",
    )(sink, proj, proj, proj, proj, bias)


GLA_TS = 512
GLA_C = 128
GLA_NC = GLA_TS // GLA_C


def _gla_chunk(q, k, v, gl, w_ref, ba_ref, st_ref, rev):
    C = GLA_C
    x = _dot(gl, w_ref[...]) + ba_ref[...]
    la = (jnp.minimum(x, 0.0) - jnp.log(1.0 + jnp.exp(-jnp.abs(x)))) * (1.0 / GLA_NORMALIZER)
    row = lax.broadcasted_iota(I32, (C, C), 0)
    col = lax.broadcasted_iota(I32, (C, C), 1)
    if rev:
        tri = jnp.where(col >= row, 1.0, 0.0).astype(BF16)
        keep = col > row
    else:
        tri = jnp.where(col <= row, 1.0, 0.0).astype(BF16)
        keep = col <= row
    hi, mid, lo3 = _split3(la)
    bc = _dot(tri, hi) + _dot(tri, mid) + _dot(tri, lo3)
    tot = bc[0:1] if rev else bc[C - 1:C]
    half = 0.5 * tot
    eh = jnp.exp(half)
    qf = q.astype(F32) * (GLA_K ** -0.5)
    kf = k.astype(F32)
    qe = qf * jnp.exp(bc - half)
    ke = kf * jnp.exp(half - bc)
    qt, kt = qe.astype(BF16), ke.astype(BF16)
    qh = (qe * eh).astype(BF16)
    kb = (ke * eh).astype(BF16)
    dec = eh * eh
    lo = lax.broadcasted_iota(I32, (1, LANES), 1) < GLA_K
    outs = []
    for m in range(GLA_HEADS // 2):
        sl = slice(m * LANES, (m + 1) * LANES)
        qtp, ktp, qhp, kbp = qt[:, sl], kt[:, sl], qh[:, sl], kb[:, sl]
        st = st_ref[m]
        stb = st.astype(BF16)
        upd = []
        for par in range(2):
            h = 2 * m + par
            msk = lo if par == 0 else jnp.logical_not(lo)
            zero = jnp.zeros_like(ktp)
            a = _dot_nt(qtp, jnp.where(msk, ktp, zero))
            a = jnp.where(keep, a, 0.0).astype(BF16)
            vh = v[:, h * GLA_V:(h + 1) * GLA_V]
            outs.append(_dot(a, vh) + _dot_nt(jnp.where(msk, qhp, zero), stb))
            upd.append(_dot_tn(vh, kbp))
        st_ref[m] = st * dec[:, sl] + jnp.where(lo, upd[0], upd[1])
    return outs


def _gla_fwd_kernel(q_ref, k_ref, v_ref, gl_ref, w_ref, ba_ref, o_ref, st_ref):
    @pl.when(pl.program_id(1) == 0)
    def _():
        st_ref[...] = jnp.zeros_like(st_ref)

    for c in range(GLA_NC):
        rs = slice(c * GLA_C, (c + 1) * GLA_C)
        outs = _gla_chunk(q_ref[0, rs, :], k_ref[0, rs, :], v_ref[0, rs, :], gl_ref[0, rs, :],
                          w_ref, ba_ref, st_ref, rev=False)
        for h in range(GLA_HEADS):
            o_ref[0, rs, h * GLA_V:(h + 1) * GLA_V] = outs[h].astype(BF16)


def _gla_bwd_kernel(q_ref, k_ref, v_ref, gl_ref, w_ref, ba_ref, of_ref, gr_ref, nw_ref, o_ref, st_ref):
    @pl.when(pl.program_id(1) == 0)
    def _():
        st_ref[...] = jnp.zeros_like(st_ref)

    for c in reversed(range(GLA_NC)):
        rs = slice(c * GLA_C, (c + 1) * GLA_C)
        outs = _gla_chunk(q_ref[0, rs, :], k_ref[0, rs, :], v_ref[0, rs, :], gl_ref[0, rs, :],
                          w_ref, ba_ref, st_ref, rev=True)
        for h in range(GLA_HEADS):
            hs = slice(h * GLA_V, (h + 1) * GLA_V)
            o = outs[h] + of_ref[0, rs, hs].astype(F32)
            g = gr_ref[0, rs, hs].astype(F32)
            o_ref[0, rs, hs] = (_rms(o, nw_ref[...]) * (g * jax.nn.sigmoid(g))).astype(BF16)


def _gla(proj, wa2_f, ba_f, wa2_b, ba_b, gla_norm):
    B, S, _ = proj.shape
    nt = S // GLA_TS
    assert S % GLA_TS == 0
    kw = GLA_HEADS * GLA_K
    wf = jnp.zeros((LANES, kw), F32).at[0:GLA_RANK].set(wa2_f).astype(BF16)
    wb = jnp.zeros((LANES, kw), F32).at[GLA_RANK:2 * GLA_RANK].set(wa2_b).astype(BF16)

    def specs(tile):
        return [pl.BlockSpec((1, GLA_TS, 256), lambda b, i: (b, tile(i), C_GQ // 256)),
                pl.BlockSpec((1, GLA_TS, 256), lambda b, i: (b, tile(i), C_GK // 256)),
                pl.BlockSpec((1, GLA_TS, 512), lambda b, i: (b, tile(i), C_GV // 512)),
                pl.BlockSpec((1, GLA_TS, LANES), lambda b, i: (b, tile(i), C_GL // LANES)),
                pl.BlockSpec((LANES, kw), lambda b, i: (0, 0)),
                pl.BlockSpec((1, kw), lambda b, i: (0, 0))]

    scratch = [pltpu.VMEM((GLA_HEADS // 2, GLA_V, LANES), F32)]
    fwd_tile = lambda i: i
    o_f = pl.pallas_call(
        _gla_fwd_kernel,
        grid=(B, nt),
        in_specs=specs(fwd_tile),
        out_specs=pl.BlockSpec((1, GLA_TS, 512), lambda b, i: (b, i, 0)),
        out_shape=jax.ShapeDtypeStruct((B, S, 512), BF16),
        scratch_shapes=scratch,
        compiler_params=_cparams(("arbitrary", "arbitrary")),
        name="gla_fwd",
    )(proj, proj, proj, proj, wf, ba_f.reshape(1, kw))
    bwd_tile = lambda i: nt - 1 - i
    return pl.pallas_call(
        _gla_bwd_kernel,
        grid=(B, nt),
        in_specs=specs(bwd_tile) + [
            pl.BlockSpec((1, GLA_TS, 512), lambda b, i: (b, bwd_tile(i), 0)),
            pl.BlockSpec((1, GLA_TS, 512), lambda b, i: (b, bwd_tile(i), C_GR // 512)),
            pl.BlockSpec((1, GLA_V), lambda b, i: (0, 0))],
        out_specs=pl.BlockSpec((1, GLA_TS, 512), lambda b, i: (b, bwd_tile(i), 0)),
        out_shape=jax.ShapeDtypeStruct((B, S, 512), BF16),
        scratch_shapes=scratch,
        compiler_params=_cparams(("arbitrary", "arbitrary")),
        name="gla_bwd",
    )(proj, proj, proj, proj, wb, ba_b.reshape(1, kw), o_f, proj, gla_norm.reshape(1, GLA_V))


def _post_kernel(x_ref, attn_ref, og_ref, ga_ref, gg_ref, ada_ref, wba_ref, wbg_ref, wo_ref, npost_ref,
                 npre_ref, wrh_ref, wrl_ref, x1_ref, h2_ref, g3_ref, afft_ref):
    ga = ga_ref[0].astype(F32)
    gg = gg_ref[0].astype(F32)
    merged = (jax.nn.sigmoid(ga) * _dot(attn_ref[0], wba_ref[...])
              + jax.nn.sigmoid(gg) * _dot(og_ref[0], wbg_ref[...]))
    mix = _dot(merged.astype(BF16), wo_ref[...])
    x1 = x_ref[0] + ada_ref[0, 2:3, :] * _rms(mix, npost_ref[...])
    x1_ref[0] = x1
    h2 = _rms(x1, npre_ref[...]) * (1.0 + ada_ref[0, 4:5, :]) + ada_ref[0, 3:4, :]
    h2_ref[0] = h2.astype(BF16)
    h_hi, h_lo = _split2(h2)
    logits = _dot(h_hi, wrh_ref[...]) + _dot(h_lo, wrh_ref[...]) + _dot(h_hi, wrl_ref[...])
    lane = lax.broadcasted_iota(I32, logits.shape, 1)
    logits = jnp.where(lane < N_EXPERTS, logits, NEG)
    e = jnp.exp(logits - jnp.max(logits, axis=-1, keepdims=True))
    aff = e / jnp.sum(e, axis=-1, keepdims=True)
    a_hi, a_mid, a_lo = _split3(aff)
    g3 = (a_hi.astype(F32) + pltpu.roll(a_mid.astype(F32), N_EXPERTS, 1)
          + pltpu.roll(a_lo.astype(F32), 2 * N_EXPERTS, 1))
    g3_ref[0] = g3.astype(BF16)
    afft_ref[...] = aff.T[0:N_EXPERTS, :]


def _post(x, attn, o_g, proj, ada, w_ba, w_bg, w_out, n_post, n_pre, wr_hi, wr_lo):
    B, S, _ = x.shape
    tm = 512
    nt = S // tm
    const = lambda b, i: (0, 0)
    return pl.pallas_call(
        _post_kernel,
        grid=(B, nt),
        in_specs=[pl.BlockSpec((1, tm, D), lambda b, i: (b, i, 0)),
                  pl.BlockSpec((1, tm, 512), lambda b, i: (b, i, 0)),
                  pl.BlockSpec((1, tm, 512), lambda b, i: (b, i, 0)),
                  pl.BlockSpec((1, tm, D), lambda b, i: (b, i, C_GA // D)),
                  pl.BlockSpec((1, tm, D), lambda b, i: (b, i, C_GG // D)),
                  pl.BlockSpec((1, 6, D), lambda b, i: (b, 0, 0)),
                  pl.BlockSpec((512, D), const),
                  pl.BlockSpec((512, D), const),
                  pl.BlockSpec((D, D), const),
                  pl.BlockSpec((1, D), const),
                  pl.BlockSpec((1, D), const),
                  pl.BlockSpec((D, LANES), const),
                  pl.BlockSpec((D, LANES), const)],
        out_specs=[pl.BlockSpec((1, tm, D), lambda b, i: (b, i, 0)),
                   pl.BlockSpec((1, tm, D), lambda b, i: (b, i, 0)),
                   pl.BlockSpec((1, tm, LANES), lambda b, i: (b, i, 0)),
                   pl.BlockSpec((N_EXPERTS, tm), lambda b, i: (0, b * nt + i))],
        out_shape=[jax.ShapeDtypeStruct((B, S, D), F32),
                   jax.ShapeDtypeStruct((B, S, D), BF16),
                   jax.ShapeDtypeStruct((B, S, LANES), BF16),
                   jax.ShapeDtypeStruct((N_EXPERTS, B * S), F32)],
        compiler_params=_cparams(("arbitrary", "arbitrary")),
        name="post",
    )(x, attn, o_g, proj, proj, ada, w_ba, w_bg, w_out, n_post.reshape(1, D), n_pre.reshape(1, D), wr_hi, wr_lo)


def _select_kernel(aff_ref, selpos_ref, post_ref, bs_ref, *, cap, n_tok):
    nblk = n_tok // LANES

    def count(mask):
        return jnp.sum(jnp.where(mask, 1.0, 0.0), axis=1, keepdims=True).astype(I32)

    def bit_body(it, thr):
        cand = thr | jnp.left_shift(jnp.int32(1), 30 - it)
        bits = pltpu.bitcast(aff_ref[...], I32)
        return jnp.where(count(bits >= cand) >= cap, cand, thr)

    thr = lax.fori_loop(0, 31, bit_body, jnp.zeros((N_EXPERTS, 1), I32))
    need = cap - count(pltpu.bitcast(aff_ref[...], I32) > thr)
    r = lax.broadcasted_iota(I32, (LANES, LANES), 0)
    c = lax.broadcasted_iota(I32, (LANES, LANES), 1)
    upper = jnp.where(r <= c, 1.0, 0.0).astype(BF16)
    blk_lane = lax.broadcasted_iota(I32, bs_ref.shape, 1)
    filler = jnp.full((LANES - N_EXPERTS, LANES), -1.0, F32)

    def blk_body(j, carry):
        c_gt, c_eq, bs = carry
        off = pl.multiple_of(j * LANES, LANES)
        b = pltpu.bitcast(aff_ref[:, pl.ds(off, LANES)], I32)
        gt = b > thr
        eq = b == thr
        both = jnp.concatenate([jnp.where(gt, 1.0, 0.0), jnp.where(eq, 1.0, 0.0)], axis=0).astype(BF16)
        incl = _dot(both, upper)
        incl_gt, incl_eq = incl[0:N_EXPERTS].astype(I32), incl[N_EXPERTS:].astype(I32)
        ex_gt = c_gt + incl_gt - jnp.where(gt, 1, 0)
        ex_eq = c_eq + incl_eq - jnp.where(eq, 1, 0)
        sel = jnp.logical_or(gt, jnp.logical_and(eq, ex_eq < need))
        pos = ex_gt + jnp.minimum(ex_eq, need)
        sp = jnp.where(sel, pos, -1)
        selpos_ref[:, pl.ds(off, LANES)] = sp
        post_ref[pl.ds(off, LANES), :] = jnp.concatenate([sp.astype(F32), filler], axis=0).T
        bs = jnp.where(blk_lane == j, c_gt + jnp.minimum(c_eq, need), bs)
        return c_gt + count(gt), c_eq + count(eq), bs

    z = jnp.zeros((N_EXPERTS, 1), I32)
    c_gt, c_eq, bs = lax.fori_loop(0, nblk, blk_body, (z, z, jnp.zeros(bs_ref.shape, I32)))
    bs_ref[...] = jnp.where(blk_lane >= nblk, c_gt + jnp.minimum(c_eq, need), bs)


def _select(aff_t, cap):
    n_tok = aff_t.shape[1]
    nbp = ((n_tok // LANES + 1 + LANES - 1) // LANES) * LANES
    return pl.pallas_call(
        functools.partial(_select_kernel, cap=cap, n_tok=n_tok),
        out_shape=[jax.ShapeDtypeStruct((N_EXPERTS, n_tok), I32),
                   jax.ShapeDtypeStruct((n_tok, LANES), F32),
                   jax.ShapeDtypeStruct((N_EXPERTS, nbp), I32)],
        compiler_params=pltpu.CompilerParams(vmem_limit_bytes=VMEM_LIMIT),
        name="select",
    )(aff_t)


def _row_window(bs_ref, e, tile):
    per = ROUTE_TILE // LANES
    start = bs_ref[e, tile * per]
    cnt = bs_ref[e, tile * per + per] - start
    base = (start // ROW_ALIGN) * ROW_ALIGN
    return start, cnt, base, start - base


def _compact_kernel(bs_ref, h2_ref, g3_ref, selpos_ref, xs_ref, stage_ref, extra_ref, carry_ref, sems, esem, pend_ref):
    t = pl.program_id(0)
    nt = pl.num_programs(0)

    def strip_copy(e, base):
        return pltpu.make_async_copy(stage_ref.at[pl.ds(e * CHUNK, CHUNK), :],
                                     xs_ref.at[e, pl.ds(pl.multiple_of(base, ROW_ALIGN), CHUNK), :], sems.at[e])

    @pl.when(t == 0)
    def _():
        extra_ref[...] = jnp.zeros_like(extra_ref)
        cap = xs_ref.shape[1] - CHUNK
        for e in range(N_EXPERTS):
            pend_ref[e] = 0
            pltpu.make_async_copy(extra_ref, xs_ref.at[e, pl.ds(cap, CHUNK), :], esem).start()
        for e in range(N_EXPERTS):
            pltpu.make_async_copy(extra_ref, xs_ref.at[e, pl.ds(cap, CHUNK), :], esem).wait()

    jrow = lax.broadcasted_iota(I32, (CHUNK, ROUTE_TILE), 0)
    pieces = []
    for e in range(N_EXPERTS):
        _, _, base, _ = _row_window(bs_ref, e, t)
        pieces.append(jnp.where(selpos_ref[e:e + 1, :] == jrow + base, 1.0, 0.0).astype(BF16))
    onehot = jnp.concatenate(pieces, axis=0)

    for e in range(N_EXPERTS):
        @pl.when(pend_ref[e] == 1)
        def _():
            strip_copy(e, 0).wait()
            pend_ref[e] = 0

    stage_ref[:, 0:D] = _dot(onehot, h2_ref[...]).astype(BF16)
    stage_ref[:, D:XS_W] = _dot(onehot, g3_ref[...]).astype(BF16)

    r16 = lax.broadcasted_iota(I32, (ROW_ALIGN, XS_W), 0)
    for e in range(N_EXPERTS):
        start, cnt, base, off = _row_window(bs_ref, e, t)

        @pl.when(cnt > 0)
        def _():
            head = pl.ds(e * CHUNK, ROW_ALIGN)
            stage_ref[head, :] = jnp.where(r16 < off, carry_ref[e], stage_ref[head, :])
            strip_copy(e, base).start()
            pend_ref[e] = 1
            end = start + cnt
            nch = (off + cnt + CHUNK - 1) // CHUNK
            gbase = (end // ROW_ALIGN) * ROW_ALIGN

            @pl.when(nch == 1)
            def _():
                @pl.when(gbase - base < CHUNK)
                def _():
                    src = pl.multiple_of(e * CHUNK + gbase - base, ROW_ALIGN)
                    carry_ref[e] = stage_ref[pl.ds(src, ROW_ALIGN), :]

            def more(cidx, _):
                cb = base + cidx * CHUNK
                oh = jnp.where(selpos_ref[e:e + 1, :] == jrow + cb, 1.0, 0.0).astype(BF16)
                extra_ref[:, 0:D] = _dot(oh, h2_ref[...]).astype(BF16)
                extra_ref[:, D:XS_W] = _dot(oh, g3_ref[...]).astype(BF16)
                cp = pltpu.make_async_copy(extra_ref, xs_ref.at[e, pl.ds(pl.multiple_of(cb, ROW_ALIGN), CHUNK), :], esem)
                cp.start()
                cp.wait()

                @pl.when(jnp.logical_and(cidx == nch - 1, gbase - cb < CHUNK))
                def _():
                    src = pl.multiple_of(gbase - cb, ROW_ALIGN)
                    carry_ref[e] = extra_ref[pl.ds(src, ROW_ALIGN), :]
                return 0

            lax.fori_loop(1, nch, more, 0)

    @pl.when(t == nt - 1)
    def _():
        for e in range(N_EXPERTS):
            @pl.when(pend_ref[e] == 1)
            def _():
                strip_copy(e, 0).wait()
                pend_ref[e] = 0


def _compact(bs, h2, g3, selpos, cap):
    n_tok = h2.shape[0]
    nt = n_tok // ROUTE_TILE
    grid_spec = pltpu.PrefetchScalarGridSpec(
        num_scalar_prefetch=1,
        grid=(nt,),
        in_specs=[pl.BlockSpec((ROUTE_TILE, D), lambda t, bs: (t, 0)),
                  pl.BlockSpec((ROUTE_TILE, LANES), lambda t, bs: (t, 0)),
                  pl.BlockSpec((N_EXPERTS, ROUTE_TILE), lambda t, bs: (0, t))],
        out_specs=pl.BlockSpec(memory_space=pl.ANY),
        scratch_shapes=[pltpu.VMEM((N_EXPERTS * CHUNK, XS_W), BF16),
                        pltpu.VMEM((CHUNK, XS_W), BF16),
                        pltpu.VMEM((N_EXPERTS, ROW_ALIGN, XS_W), BF16),
                        pltpu.SemaphoreType.DMA((N_EXPERTS,)),
                        pltpu.SemaphoreType.DMA(()),
                        pltpu.SMEM((N_EXPERTS,), I32)])
    return pl.pallas_call(
        _compact_kernel,
        grid_spec=grid_spec,
        out_shape=jax.ShapeDtypeStruct((N_EXPERTS, cap + CHUNK, XS_W), BF16),
        compiler_params=_cparams(("arbitrary",)),
        name="compact",
    )(bs, h2, g3, selpos)


def _ffn_kernel(xs_ref, wg_ref, wu_ref, wd_ref, ys_ref):
    e = pl.program_id(0)
    xs = xs_ref[0]
    x = xs[:, 0:D]
    lane = lax.broadcasted_iota(I32, (1, LANES), 1)
    mine = jnp.logical_or(lane == e, jnp.logical_or(lane == e + N_EXPERTS, lane == e + 2 * N_EXPERTS))
    gate = jnp.sum(jnp.where(mine, xs[:, D:XS_W].astype(F32), 0.0), axis=1, keepdims=True)
    a = _dot(x, wg_ref[0])
    hid = (a * jax.nn.sigmoid(a) * _dot(x, wu_ref[0])).astype(BF16)
    ys_ref[0] = (_dot(hid, wd_ref[0]) * gate).astype(BF16)


def _ffn(xs, w_gate, w_up, w_down, cap):
    tm = min(512, cap)
    assert cap % tm == 0
    wspec = lambda shape: pl.BlockSpec(shape, lambda e, j: (e, 0, 0))
    return pl.pallas_call(
        _ffn_kernel,
        grid=(N_EXPERTS, cap // tm),
        in_specs=[pl.BlockSpec((1, tm, XS_W), lambda e, j: (e, j, 0)),
                  wspec((1, D, FF)), wspec((1, D, FF)), wspec((1, FF, D))],
        out_specs=pl.BlockSpec((1, tm, D), lambda e, j: (e, j, 0)),
        out_shape=jax.ShapeDtypeStruct((N_EXPERTS, cap, D), BF16),
        compiler_params=_cparams(("arbitrary", "arbitrary")),
        name="ffn",
    )(xs, w_gate, w_up, w_down)


def _combine_kernel(bs_ref, x1_ref, post_ref, ada_ref, nw_ref, expand_ref, ys_ref, o_ref,
                    strips_ref, extra_ref, acc_ref, sems, esem, *, cap):
    nts = pl.num_programs(1)
    t = pl.program_id(0) * nts + pl.program_id(1)
    nt = pl.num_programs(0) * nts

    def strip_base(e, tile):
        _, _, base, _ = _row_window(bs_ref, e, tile)
        return jnp.minimum(base, cap - CHUNK)

    def strip_copy(e, tile, slot):
        src = ys_ref.at[e, pl.ds(pl.multiple_of(strip_base(e, tile), ROW_ALIGN), CHUNK), :]
        return pltpu.make_async_copy(src, strips_ref.at[slot, pl.ds(e * CHUNK, CHUNK), :], sems.at[slot, e])

    @pl.when(t == 0)
    def _():
        for e in range(N_EXPERTS):
            strip_copy(e, 0, 0).start()

    @pl.when(t + 1 < nt)
    def _():
        for e in range(N_EXPERTS):
            strip_copy(e, t + 1, (t + 1) % 2).start()

    pos = post_ref[...]
    p_hi = jnp.floor(pos * (1.0 / CHUNK))
    p_lo = pos - p_hi * CHUNK
    hi_x = _dot(p_hi.astype(BF16), expand_ref[...])
    lo_x = _dot(p_lo.astype(BF16), expand_ref[...])
    lane = lax.broadcasted_iota(I32, (1, N_EXPERTS * CHUNK), 1)
    want = lane % CHUNK
    for e in range(N_EXPERTS):
        want = want + jnp.where(lane // CHUNK == e, strip_base(e, t), 0)
    w_hi = (want // CHUNK).astype(F32)
    w_lo = (want % CHUNK).astype(F32)
    onehot = jnp.where(jnp.logical_and(hi_x == w_hi, lo_x == w_lo), 1.0, 0.0).astype(BF16)

    slot = t % 2
    for e in range(N_EXPERTS):
        strip_copy(e, t, slot).wait()
    acc_ref[...] = _dot(onehot, strips_ref[slot])

    for e in range(N_EXPERTS):
        start, cnt, base, off = _row_window(bs_ref, e, t)
        nch = (off + cnt + CHUNK - 1) // CHUNK

        def more(cidx, _):
            cb = jnp.minimum(base + cidx * CHUNK, cap - CHUNK)
            cp = pltpu.make_async_copy(ys_ref.at[e, pl.ds(pl.multiple_of(cb, ROW_ALIGN), CHUNK), :], extra_ref, esem)
            cp.start()
            tgt = (lax.broadcasted_iota(I32, (1, CHUNK), 1) + cb).astype(F32)
            prev_end = (jnp.minimum(base + (cidx - 1) * CHUNK, cap - CHUNK) + CHUNK).astype(F32)
            col = pos[:, e:e + 1]
            oh = jnp.where(jnp.logical_and(col == tgt, col >= prev_end), 1.0, 0.0).astype(BF16)
            cp.wait()
            acc_ref[...] += _dot(oh, extra_ref[...])
            return 0

        lax.fori_loop(1, nch, more, 0)

    y = acc_ref[...]
    o_ref[0] = x1_ref[0] + ada_ref[0, 5:6, :] * _rms(y, nw_ref[...])


def _combine(bs, x1, post, ada, norm_w, expand, ys, cap):
    B, S, _ = x1.shape
    nts = S // ROUTE_TILE
    grid_spec = pltpu.PrefetchScalarGridSpec(
        num_scalar_prefetch=1,
        grid=(B, nts),
        in_specs=[pl.BlockSpec((1, ROUTE_TILE, D), lambda b, i, bs: (b, i, 0)),
                  pl.BlockSpec((ROUTE_TILE, LANES), lambda b, i, bs: (b * nts + i, 0)),
                  pl.BlockSpec((1, 6, D), lambda b, i, bs: (b, 0, 0)),
                  pl.BlockSpec((1, D), lambda b, i, bs: (0, 0)),
                  pl.BlockSpec((LANES, N_EXPERTS * CHUNK), lambda b, i, bs: (0, 0)),
                  pl.BlockSpec(memory_space=pl.ANY)],
        out_specs=pl.BlockSpec((1, ROUTE_TILE, D), lambda b, i, bs: (b, i, 0)),
        scratch_shapes=[pltpu.VMEM((2, N_EXPERTS * CHUNK, D), BF16),
                        pltpu.VMEM((CHUNK, D), BF16),
                        pltpu.VMEM((ROUTE_TILE, D), F32),
                        pltpu.SemaphoreType.DMA((2, N_EXPERTS)),
                        pltpu.SemaphoreType.DMA(())])
    return pl.pallas_call(
        functools.partial(_combine_kernel, cap=cap),
        grid_spec=grid_spec,
        out_shape=jax.ShapeDtypeStruct((B, S, D), F32),
        compiler_params=_cparams(("arbitrary", "arbitrary")),
        name="combine",
    )(bs, x1, post, ada, norm_w.reshape(1, D), expand, ys)


def _prepare_weights(w_in, w_branch_attn, w_branch_gla, w_out, w_router, w_gate_e, w_up_e, w_down_e):
    aq, ak, av, gq, gk, gv, gr, glf, glb, ga, gg = jnp.split(
        w_in, np.cumsum([512, 128, 128, 256, 256, 512, 512, 16, 16, 1024]).tolist(), axis=1)
    pad = jnp.zeros((D, NP - C_GL - 2 * GLA_RANK), F32)
    w_in_p = jnp.concatenate([aq, gv, ga, gg, gr, ak, av, gq, gk, glf, glb, pad], axis=1).astype(BF16)
    wr = jnp.zeros((D, LANES), F32).at[:, 0:N_EXPERTS].set(w_router)
    wr_hi = wr.astype(BF16)
    wr_lo = (wr - wr_hi.astype(F32)).astype(BF16)
    expand = np.zeros((LANES, N_EXPERTS * CHUNK), np.float32)
    for e in range(N_EXPERTS):
        expand[e, e * CHUNK:(e + 1) * CHUNK] = 1.0
    return dict(w_in_p=w_in_p, w_ba=w_branch_attn.astype(BF16), w_bg=w_branch_gla.astype(BF16),
                w_out=w_out.astype(BF16), wr_hi=wr_hi, wr_lo=wr_lo, w_gate=w_gate_e.astype(BF16),
                w_up=w_up_e.astype(BF16), w_down=w_down_e.astype(BF16), expand=jnp.asarray(expand, BF16),
                bias=_attn_bias_table())


def _layer(x, ada, w, p):
    B, S, _ = x.shape
    n_tok = B * S
    cap = CAPACITY_FACTOR * n_tok // N_EXPERTS
    assert n_tok % ROUTE_TILE == 0 and S % ROUTE_TILE == 0 and cap % CHUNK == 0
    proj = _inproj(x, ada, p['norm_pre_mix'], w['w_in_p'])
    attn = _attention(proj, p['attn_sink'], w['bias'])
    o_g = _gla(proj, p['gla_wa2_fwd'], p['gla_ba_fwd'], p['gla_wa2_bwd'], p['gla_ba_bwd'], p['gla_norm'])
    x1, h2, g3, aff_t = _post(x, attn, o_g, proj, ada, w['w_ba'], w['w_bg'], w['w_out'],
                              p['norm_post_mix'], p['norm_pre_ffn'], w['wr_hi'], w['wr_lo'])
    selpos, post, bs = _select(aff_t, cap)
    xs = _compact(bs, h2.reshape(n_tok, D), g3.reshape(n_tok, LANES), selpos, cap)
    ys = _ffn(xs, w['w_gate'], w['w_up'], w['w_down'], cap)
    return _combine(bs, x1, post, ada, p['norm_post_ffn'], w['expand'], ys, cap)


def kernel(x_prompt, x_sample, c_prompt, c_sample, w_ada, b_ada, norm_pre_mix, norm_post_mix, w_in, attn_sink, gla_wa2_fwd, gla_ba_fwd, gla_wa2_bwd, gla_ba_bwd, gla_norm, w_branch_attn, w_branch_gla, w_out, norm_pre_ffn, norm_post_ffn, w_router, w_gate_e, w_up_e, w_down_e):
    assert w_ada.shape[0] == 1, "one layer"
    p = dict(norm_pre_mix=norm_pre_mix[0], norm_post_mix=norm_post_mix[0], attn_sink=attn_sink[0],
             gla_wa2_fwd=gla_wa2_fwd[0], gla_ba_fwd=gla_ba_fwd[0], gla_wa2_bwd=gla_wa2_bwd[0],
             gla_ba_bwd=gla_ba_bwd[0], gla_norm=gla_norm[0], norm_pre_ffn=norm_pre_ffn[0],
             norm_post_ffn=norm_post_ffn[0])
    w = _prepare_weights(w_in[0], w_branch_attn[0], w_branch_gla[0], w_out[0], w_router[0],
                         w_gate_e[0], w_up_e[0], w_down_e[0])
    bp, bs_ = c_prompt.shape[0], c_sample.shape[0]
    rows = -(-(bp + bs_) // 8) * 8
    c_all = jnp.concatenate([c_prompt, c_sample, jnp.zeros((rows - bp - bs_, D), F32)], axis=0)
    ada = _ada(c_all, w_ada[0], b_ada[0]).reshape(rows, 6, D)
    y_prompt = _layer(x_prompt, ada[0:bp], w, p)
    y_sample = _layer(x_sample, ada[bp:bp + bs_], w, p)
    return (y_prompt, y_sample)
```

```python
import functools

import numpy as np
import jax
import jax.numpy as jnp
from jax import lax
from jax.experimental import pallas as pl
from jax.experimental.pallas import tpu as pltpu

F32, BF16, I32 = jnp.float32, jnp.bfloat16, jnp.int32

D = 1024
N_Q_HEADS, N_KV_HEADS, HEAD_DIM = 8, 2, 64
ATTN_BLOCK = 128
GLA_HEADS, GLA_K, GLA_V = 4, 64, 128
GLA_RANK = 16
GLA_NORMALIZER = 16.0
N_EXPERTS, FF = 16, 1024
CAPACITY_FACTOR = 2
EPS = 1e-6
NEG = -1e30

C_AQ, C_GV, C_GA, C_GG, C_GR, C_KV, C_GQ, C_GK, C_GL = 0, 512, 1024, 2048, 3072, 3584, 3840, 4096, 4352
NP = 4480

LANES = 128
ROW_ALIGN = 16
CHUNK = 64
ROUTE_TILE = 256
XS_W = D + LANES
VMEM_LIMIT = 56 * 1024 * 1024


def _cparams(sem):
    return pltpu.CompilerParams(dimension_semantics=sem, vmem_limit_bytes=VMEM_LIMIT)


def _dot(a, b):
    return jnp.dot(a, b, preferred_element_type=F32)


def _dot_nt(a, b):
    return lax.dot_general(a, b, (((1,), (1,)), ((), ())), preferred_element_type=F32)


def _dot_tn(a, b):
    return lax.dot_general(a, b, (((0,), (0,)), ((), ())), preferred_element_type=F32)


def _split2(x):
    hi = x.astype(BF16)
    lo = (x - hi.astype(F32)).astype(BF16)
    return hi, lo


def _split3(x):
    hi = x.astype(BF16)
    r = x - hi.astype(F32)
    mid = r.astype(BF16)
    lo = (r - mid.astype(F32)).astype(BF16)
    return hi, mid, lo


def _sigmoid(x):
    return 0.5 * jnp.tanh(0.5 * x) + 0.5


def _rms(x, w):
    return x * lax.rsqrt(jnp.mean(x * x, axis=-1, keepdims=True) + EPS) * w


def _ada_kernel(c_ref, w_ref, b_ref, o_ref):
    c = c_ref[...]
    s = c * jax.nn.sigmoid(c)
    s_hi, s_lo = _split2(s)
    w_hi, w_lo = _split2(w_ref[...])
    o_ref[...] = _dot(s_hi, w_hi) + _dot(s_lo, w_hi) + _dot(s_hi, w_lo) + b_ref[...]


def _ada(c, w_ada, b_ada):
    rows = c.shape[0]
    tn = 1024
    return pl.pallas_call(
        _ada_kernel,
        grid=(6 * D // tn,),
        in_specs=[pl.BlockSpec((rows, D), lambda j: (0, 0)),
                  pl.BlockSpec((D, tn), lambda j: (0, j)),
                  pl.BlockSpec((1, tn), lambda j: (0, j))],
        out_specs=pl.BlockSpec((rows, tn), lambda j: (0, j)),
        out_shape=jax.ShapeDtypeStruct((rows, 6 * D), F32),
        compiler_params=_cparams(("arbitrary",)),
        name="ada",
    )(c, w_ada, b_ada.reshape(1, 6 * D))


def _inproj_kernel(x_ref, ada_ref, nw_ref, w_ref, o_ref):
    x = x_ref[0]
    h = (_rms(x, nw_ref[...] * (1.0 + ada_ref[0, 1:2, :])) + ada_ref[0, 0:1, :]).astype(BF16)
    step = 1024
    for c0 in range(0, NP, step):
        c1 = min(c0 + step, NP)
        o_ref[0, :, c0:c1] = _dot(h, w_ref[:, c0:c1]).astype(BF16)


def _inproj(x, ada, norm_w, w_in_p):
    B, S, _ = x.shape
    tm = 512
    return pl.pallas_call(
        _inproj_kernel,
        grid=(B, S // tm),
        in_specs=[pl.BlockSpec((1, tm, D), lambda b, i: (b, i, 0)),
                  pl.BlockSpec((1, 6, D), lambda b, i: (b, 0, 0)),
                  pl.BlockSpec((1, D), lambda b, i: (0, 0)),
                  pl.BlockSpec((D, NP), lambda b, i: (0, 0))],
        out_specs=pl.BlockSpec((1, tm, NP), lambda b, i: (b, i, 0)),
        out_shape=jax.ShapeDtypeStruct((B, S, NP), BF16),
        compiler_params=_cparams(("arbitrary", "arbitrary")),
        name="inproj",
    )(x, ada, norm_w.reshape(1, D), w_in_p)


ATTN_TQ = 512
ATTN_QB = ATTN_TQ // ATTN_BLOCK


_ATTN_STACK = (0, 2, 1, 3)


def _attn_bias_table():
    L = ATTN_BLOCK
    group = N_Q_HEADS // N_KV_HEADS
    q = np.arange(L)[:, None]
    k = np.arange(3 * L)[None, :]
    dist = np.abs(q - k + L).astype(np.float32)
    slopes = np.asarray([2.0 ** (-8.0 * (h + 1) / N_Q_HEADS) for h in range(N_Q_HEADS)], np.float32)
    base = -slopes[:, None, None] * dist[None]
    inside = (dist <= L)[None]
    variants = []
    for ok_k in (np.ones_like(k, bool), k >= L, k < 2 * L):
        per_head = np.where(inside & ok_k[None], base, np.float32(NEG))
        variants.append(np.stack([np.concatenate([per_head[hkv * group + g] for g in _ATTN_STACK], axis=0)
                                  for hkv in range(N_KV_HEADS)]))
    return jnp.asarray(np.stack(variants), F32)


def _attn_kernel(sink_ref, q_ref, kvm_ref, kvp_ref, kvn_ref, bias_ref, o_ref):
    L = ATTN_BLOCK
    i = pl.program_id(1)
    last = pl.num_programs(1) - 1
    kv = jnp.concatenate([kvp_ref[0], kvm_ref[0], kvn_ref[0]], axis=0)
    k = kv[:, :LANES] * jnp.asarray(HEAD_DIM ** -0.5, BF16)
    v = kv[:, LANES:]
    lo = lax.broadcasted_iota(I32, (1, LANES), 1) < HEAD_DIM
    zero = jnp.zeros_like(k)

    def halves(t):
        sw = jnp.concatenate([t[:, HEAD_DIM:], t[:, :HEAD_DIM]], axis=1)
        return ((jnp.where(lo, t, zero), jnp.where(lo, zero, sw)),
                (jnp.where(lo, sw, zero), jnp.where(lo, zero, t)))

    kh, vh = halves(k), halves(v)
    group = N_Q_HEADS // N_KV_HEADS
    rowblk = lax.broadcasted_iota(I32, (group * L, 1), 0) // L
    for j in range(ATTN_QB):
        first = jnp.logical_and(i == 0, j == 0)
        final = jnp.logical_and(i == last, j == ATTN_QB - 1)
        var = jnp.where(first, 1, jnp.where(final, 2, 0))
        r0 = j * L
        for hkv in range(N_KV_HEADS):
            c0 = hkv * 2 * LANES
            qs = jnp.concatenate([q_ref[0, r0:r0 + L, c0:c0 + LANES],
                                  q_ref[0, r0:r0 + L, c0 + LANES:c0 + 2 * LANES]], axis=0)
            s = jnp.concatenate([_dot_nt(qs, kh[hkv][0][r0:r0 + 3 * L]),
                                 _dot_nt(qs, kh[hkv][1][r0:r0 + 3 * L])], axis=0) + bias_ref[var, hkv]
            snk = jnp.zeros((group * L, 1), F32)
            for b, g in enumerate(_ATTN_STACK):
                snk = jnp.where(rowblk == b, sink_ref[hkv * group + g], snk)
            mx = jnp.maximum(jnp.max(s, axis=-1, keepdims=True), snk)
            p = jnp.exp(s - mx)
            rden = 1.0 / (jnp.sum(p, axis=-1, keepdims=True) + jnp.exp(snk - mx))
            pb = p.astype(BF16)
            o = (_dot(pb[0:2 * L], vh[hkv][0][r0:r0 + 3 * L]) * rden[0:2 * L]
                 + _dot(pb[2 * L:4 * L], vh[hkv][1][r0:r0 + 3 * L]) * rden[2 * L:4 * L])
            o_ref[0, r0:r0 + L, c0:c0 + LANES] = o[0:L].astype(BF16)
            o_ref[0, r0:r0 + L, c0 + LANES:c0 + 2 * LANES] = o[L:2 * L].astype(BF16)


def _attention(proj, sink, bias):
    B, S, _ = proj.shape
    L = ATTN_BLOCK
    nb = S // L
    assert S % ATTN_TQ == 0 and nb >= 2
    kvc = C_KV // 256
    return pl.pallas_call(
        _attn_kernel,
        grid=(B, S // ATTN_TQ),
        in_specs=[pl.BlockSpec(memory_space=pltpu.SMEM),
                  pl.BlockSpec((1, ATTN_TQ, 512), lambda b, i: (b, i, C_AQ // 512)),
                  pl.BlockSpec((1, ATTN_TQ, 256), lambda b, i: (b, i, kvc)),
                  pl.BlockSpec((1, L, 256), lambda b, i: (b, jnp.maximum(i * ATTN_QB - 1, 0), kvc)),
                  pl.BlockSpec((1, L, 256), lambda b, i: (b, jnp.minimum(i * ATTN_QB + ATTN_QB, nb - 1), kvc)),
                  pl.BlockSpec((3, N_KV_HEADS, (N_Q_HEADS // N_KV_HEADS) * L, 3 * L), lambda b, i: (0, 0, 0, 0))],
        out_specs=pl.BlockSpec((1, ATTN_TQ, 512), lambda b, i: (b, i, 0)),
        out_shape=jax.ShapeDtypeStruct((B, S, 512), BF16),
        compiler_params=_cparams(("arbitrary", "arbitrary")),
        name="attn",
    )(sink, proj, proj, proj, proj, bias)


GLA_TS = 512
GLA_C = 128
GLA_NC = GLA_TS // GLA_C


def _gla_tile(q_ref, k_ref, v_ref, gl_ref, w_ref, ba_ref, st_ref, rev):
    C = GLA_C
    x = _dot(gl_ref[0], w_ref[...]) + ba_ref[...]
    la = (jnp.minimum(x, 0.0) - jnp.log(1.0 + jnp.exp(-jnp.abs(x)))) * (1.0 / GLA_NORMALIZER)
    row = lax.broadcasted_iota(I32, (C, C), 0)
    col = lax.broadcasted_iota(I32, (C, C), 1)
    if rev:
        tri = jnp.where(col >= row, 1.0, 0.0).astype(BF16)
        keep = col > row
    else:
        tri = jnp.where(col <= row, 1.0, 0.0).astype(BF16)
        keep = col <= row
    hi, mid, lo3 = _split3(la)
    bcs, halves, ehs = [], [], []
    for c in range(GLA_NC):
        rs = slice(c * C, (c + 1) * C)
        bc = _dot(tri, hi[rs]) + _dot(tri, mid[rs]) + _dot(tri, lo3[rs])
        half = 0.5 * (bc[0:1] if rev else bc[C - 1:C])
        bcs.append(bc)
        halves.append(jnp.broadcast_to(half, bc.shape))
        ehs.append(jnp.exp(half))
    bc = jnp.concatenate(bcs, axis=0)
    half = jnp.concatenate(halves, axis=0)
    eh = jnp.concatenate([jnp.broadcast_to(e, (C, e.shape[1])) for e in ehs], axis=0)
    qe = q_ref[0].astype(F32) * (GLA_K ** -0.5) * jnp.exp(bc - half)
    ke = k_ref[0].astype(F32) * jnp.exp(half - bc)
    qt, kt = qe.astype(BF16), ke.astype(BF16)
    qh = (qe * eh).astype(BF16)
    kb = (ke * eh).astype(BF16)
    lo = lax.broadcasted_iota(I32, (1, LANES), 1) < GLA_K
    zero = jnp.zeros((GLA_TS, LANES), BF16)
    order = range(GLA_NC - 1, -1, -1) if rev else range(GLA_NC)
    outs = [[None] * GLA_HEADS for _ in range(GLA_NC)]
    for m in range(GLA_HEADS // 2):
        sl = slice(m * LANES, (m + 1) * LANES)
        ktm = (jnp.where(lo, kt[:, sl], zero), jnp.where(lo, zero, kt[:, sl]))
        qhm = (jnp.where(lo, qh[:, sl], zero), jnp.where(lo, zero, qh[:, sl]))
        st = st_ref[m]
        for c in order:
            rs = slice(c * C, (c + 1) * C)
            stb = st.astype(BF16)
            upd = []
            for par in range(2):
                h = 2 * m + par
                a = jnp.where(keep, _dot_nt(qt[rs, sl], ktm[par][rs]), 0.0).astype(BF16)
                vh = v_ref[0, rs, h * GLA_V:(h + 1) * GLA_V]
                outs[c][h] = _dot(a, vh) + _dot_nt(qhm[par][rs], stb)
                upd.append(_dot_tn(vh, kb[rs, sl]))
            dec = ehs[c][:, sl] * ehs[c][:, sl]
            st = st * dec + jnp.where(lo, upd[0], upd[1])
        st_ref[m] = st
    return outs


def _gla_fwd_kernel(q_ref, k_ref, v_ref, gl_ref, w_ref, ba_ref, o_ref, st_ref):
    @pl.when(pl.program_id(1) == 0)
    def _():
        st_ref[...] = jnp.zeros_like(st_ref)

    outs = _gla_tile(q_ref, k_ref, v_ref, gl_ref, w_ref, ba_ref, st_ref, rev=False)
    for c in range(GLA_NC):
        for h in range(GLA_HEADS):
            o_ref[0, c * GLA_C:(c + 1) * GLA_C, h * GLA_V:(h + 1) * GLA_V] = outs[c][h].astype(BF16)


def _gla_bwd_kernel(q_ref, k_ref, v_ref, gl_ref, w_ref, ba_ref, of_ref, gr_ref, nw_ref, o_ref, st_ref):
    @pl.when(pl.program_id(1) == 0)
    def _():
        st_ref[...] = jnp.zeros_like(st_ref)

    outs = _gla_tile(q_ref, k_ref, v_ref, gl_ref, w_ref, ba_ref, st_ref, rev=True)
    for h in range(GLA_HEADS):
        hs = slice(h * GLA_V, (h + 1) * GLA_V)
        o = jnp.concatenate([outs[c][h] for c in range(GLA_NC)], axis=0) + of_ref[0, :, hs].astype(F32)
        g = gr_ref[0, :, hs].astype(F32)
        o_ref[0, :, hs] = (_rms(o, nw_ref[...]) * (g * _sigmoid(g))).astype(BF16)


def _gla(proj, wa2_f, ba_f, wa2_b, ba_b, gla_norm):
    B, S, _ = proj.shape
    nt = S // GLA_TS
    assert S % GLA_TS == 0
    kw = GLA_HEADS * GLA_K
    wf = jnp.zeros((LANES, kw), F32).at[0:GLA_RANK].set(wa2_f).astype(BF16)
    wb = jnp.zeros((LANES, kw), F32).at[GLA_RANK:2 * GLA_RANK].set(wa2_b).astype(BF16)

    def specs(tile):
        return [pl.BlockSpec((1, GLA_TS, 256), lambda b, i: (b, tile(i), C_GQ // 256)),
                pl.BlockSpec((1, GLA_TS, 256), lambda b, i: (b, tile(i), C_GK // 256)),
                pl.BlockSpec((1, GLA_TS, 512), lambda b, i: (b, tile(i), C_GV // 512)),
                pl.BlockSpec((1, GLA_TS, LANES), lambda b, i: (b, tile(i), C_GL // LANES)),
                pl.BlockSpec((LANES, kw), lambda b, i: (0, 0)),
                pl.BlockSpec((1, kw), lambda b, i: (0, 0))]

    scratch = [pltpu.VMEM((GLA_HEADS // 2, GLA_V, LANES), F32)]
    fwd_tile = lambda i: i
    o_f = pl.pallas_call(
        _gla_fwd_kernel,
        grid=(B, nt),
        in_specs=specs(fwd_tile),
        out_specs=pl.BlockSpec((1, GLA_TS, 512), lambda b, i: (b, i, 0)),
        out_shape=jax.ShapeDtypeStruct((B, S, 512), BF16),
        scratch_shapes=scratch,
        compiler_params=_cparams(("arbitrary", "arbitrary")),
        name="gla_fwd",
    )(proj, proj, proj, proj, wf, ba_f.reshape(1, kw))
    bwd_tile = lambda i: nt - 1 - i
    return pl.pallas_call(
        _gla_bwd_kernel,
        grid=(B, nt),
        in_specs=specs(bwd_tile) + [
            pl.BlockSpec((1, GLA_TS, 512), lambda b, i: (b, bwd_tile(i), 0)),
            pl.BlockSpec((1, GLA_TS, 512), lambda b, i: (b, bwd_tile(i), C_GR // 512)),
            pl.BlockSpec((1, GLA_V), lambda b, i: (0, 0))],
        out_specs=pl.BlockSpec((1, GLA_TS, 512), lambda b, i: (b, bwd_tile(i), 0)),
        out_shape=jax.ShapeDtypeStruct((B, S, 512), BF16),
        scratch_shapes=scratch,
        compiler_params=_cparams(("arbitrary", "arbitrary")),
        name="gla_bwd",
    )(proj, proj, proj, proj, wb, ba_b.reshape(1, kw), o_f, proj, gla_norm.reshape(1, GLA_V))


POST_SUB = 256

def _post_kernel(x_ref, attn_ref, og_ref, ga_ref, gg_ref, ada_ref, wba_ref, wbg_ref, wo_ref, npost_ref,
                 npre_ref, wr_ref, x1_ref, h2_ref, g3_ref, afft_ref):
    w_post = ada_ref[0, 2:3, :] * npost_ref[...]
    w_pre = (1.0 + ada_ref[0, 4:5, :]) * npre_ref[...]
    shift = ada_ref[0, 3:4, :]
    for r0 in range(0, x_ref.shape[1], POST_SUB):
        rs = slice(r0, r0 + POST_SUB)
        ga = ga_ref[0, rs, :].astype(F32)
        gg = gg_ref[0, rs, :].astype(F32)
        merged = (_sigmoid(ga) * _dot(attn_ref[0, rs, :], wba_ref[...])
                  + _sigmoid(gg) * _dot(og_ref[0, rs, :], wbg_ref[...]))
        mix = _dot(merged.astype(BF16), wo_ref[...])
        x1 = x_ref[0, rs, :] + _rms(mix, w_post)
        x1_ref[0, rs, :] = x1
        h2 = _rms(x1, w_pre) + shift
        h2_ref[0, rs, :] = h2.astype(BF16)
        h_hi, h_lo = _split2(h2)
        r = _dot(h_hi, wr_ref[...]) + _dot(h_lo, wr_ref[...])
        logits = r[:, 0:LANES] + r[:, LANES:2 * LANES]
        lane = lax.broadcasted_iota(I32, logits.shape, 1)
        logits = jnp.where(lane < N_EXPERTS, logits, NEG)
        e = jnp.exp(logits - jnp.max(logits, axis=-1, keepdims=True))
        aff = e / jnp.sum(e, axis=-1, keepdims=True)
        a_hi, a_mid, a_lo = _split3(aff)
        g3 = (a_hi.astype(F32) + pltpu.roll(a_mid.astype(F32), N_EXPERTS, 1)
              + pltpu.roll(a_lo.astype(F32), 2 * N_EXPERTS, 1))
        g3_ref[0, rs, :] = g3.astype(BF16)
        afft_ref[:, rs] = aff.T[0:N_EXPERTS, :]


def _post(x, attn, o_g, proj, ada, w_ba, w_bg, w_out, n_post, n_pre, wr):
    B, S, _ = x.shape
    tm = 512
    nt = S // tm
    const = lambda b, i: (0, 0)
    return pl.pallas_call(
        _post_kernel,
        grid=(B, nt),
        in_specs=[pl.BlockSpec((1, tm, D), lambda b, i: (b, i, 0)),
                  pl.BlockSpec((1, tm, 512), lambda b, i: (b, i, 0)),
                  pl.BlockSpec((1, tm, 512), lambda b, i: (b, i, 0)),
                  pl.BlockSpec((1, tm, D), lambda b, i: (b, i, C_GA // D)),
                  pl.BlockSpec((1, tm, D), lambda b, i: (b, i, C_GG // D)),
                  pl.BlockSpec((1, 6, D), lambda b, i: (b, 0, 0)),
                  pl.BlockSpec((512, D), const),
                  pl.BlockSpec((512, D), const),
                  pl.BlockSpec((D, D), const),
                  pl.BlockSpec((1, D), const),
                  pl.BlockSpec((1, D), const),
                  pl.BlockSpec((D, 2 * LANES), const)],
        out_specs=[pl.BlockSpec((1, tm, D), lambda b, i: (b, i, 0)),
                   pl.BlockSpec((1, tm, D), lambda b, i: (b, i, 0)),
                   pl.BlockSpec((1, tm, LANES), lambda b, i: (b, i, 0)),
                   pl.BlockSpec((N_EXPERTS, tm), lambda b, i: (0, b * nt + i))],
        out_shape=[jax.ShapeDtypeStruct((B, S, D), F32),
                   jax.ShapeDtypeStruct((B, S, D), BF16),
                   jax.ShapeDtypeStruct((B, S, LANES), BF16),
                   jax.ShapeDtypeStruct((N_EXPERTS, B * S), F32)],
        compiler_params=_cparams(("arbitrary", "arbitrary")),
        name="post",
    )(x, attn, o_g, proj, proj, ada, w_ba, w_bg, w_out, n_post.reshape(1, D), n_pre.reshape(1, D), wr)


def _select_kernel(aff_ref, selpos_ref, post_ref, bs_ref, *, cap, n_tok):
    nblk = n_tok // LANES

    def count(mask):
        return jnp.sum(jnp.where(mask, 1.0, 0.0), axis=1, keepdims=True).astype(I32)

    def bit_body(it, thr):
        cand = thr | jnp.left_shift(jnp.int32(1), 30 - it)
        bits = pltpu.bitcast(aff_ref[...], I32)
        return jnp.where(count(bits >= cand) >= cap, cand, thr)

    thr = lax.fori_loop(0, 31, bit_body, jnp.zeros((N_EXPERTS, 1), I32))
    need = cap - count(pltpu.bitcast(aff_ref[...], I32) > thr)
    r = lax.broadcasted_iota(I32, (LANES, LANES), 0)
    c = lax.broadcasted_iota(I32, (LANES, LANES), 1)
    upper = jnp.where(r <= c, 1.0, 0.0).astype(BF16)
    blk_lane = lax.broadcasted_iota(I32, bs_ref.shape, 1)
    filler = jnp.full((LANES - N_EXPERTS, LANES), -1.0, F32)

    def blk_body(j, carry):
        c_gt, c_eq, bs = carry
        off = pl.multiple_of(j * LANES, LANES)
        b = pltpu.bitcast(aff_ref[:, pl.ds(off, LANES)], I32)
        gt = b > thr
        eq = b == thr
        both = jnp.concatenate([jnp.where(gt, 1.0, 0.0), jnp.where(eq, 1.0, 0.0)], axis=0).astype(BF16)
        incl = _dot(both, upper)
        incl_gt, incl_eq = incl[0:N_EXPERTS].astype(I32), incl[N_EXPERTS:].astype(I32)
        ex_gt = c_gt + incl_gt - jnp.where(gt, 1, 0)
        ex_eq = c_eq + incl_eq - jnp.where(eq, 1, 0)
        sel = jnp.logical_or(gt, jnp.logical_and(eq, ex_eq < need))
        pos = ex_gt + jnp.minimum(ex_eq, need)
        sp = jnp.where(sel, pos, -1)
        selpos_ref[:, pl.ds(off, LANES)] = sp
        post_ref[pl.ds(off, LANES), :] = jnp.concatenate([sp.astype(F32), filler], axis=0).T
        bs = jnp.where(blk_lane == j, c_gt + jnp.minimum(c_eq, need), bs)
        return c_gt + count(gt), c_eq + count(eq), bs

    z = jnp.zeros((N_EXPERTS, 1), I32)
    c_gt, c_eq, bs = lax.fori_loop(0, nblk, blk_body, (z, z, jnp.zeros(bs_ref.shape, I32)))
    bs_ref[...] = jnp.where(blk_lane >= nblk, c_gt + jnp.minimum(c_eq, need), bs)


def _select(aff_t, cap):
    n_tok = aff_t.shape[1]
    nbp = ((n_tok // LANES + 1 + LANES - 1) // LANES) * LANES
    return pl.pallas_call(
        functools.partial(_select_kernel, cap=cap, n_tok=n_tok),
        out_shape=[jax.ShapeDtypeStruct((N_EXPERTS, n_tok), I32),
                   jax.ShapeDtypeStruct((n_tok, LANES), F32),
                   jax.ShapeDtypeStruct((N_EXPERTS, nbp), I32)],
        compiler_params=pltpu.CompilerParams(vmem_limit_bytes=VMEM_LIMIT),
        name="select",
    )(aff_t)


def _row_window(bs_ref, e, tile):
    per = ROUTE_TILE // LANES
    start = bs_ref[e, tile * per]
    cnt = bs_ref[e, tile * per + per] - start
    base = (start // ROW_ALIGN) * ROW_ALIGN
    return start, cnt, base, start - base


def _compact_kernel(bs_ref, h2_ref, g3_ref, selpos_ref, xs_ref, stage_ref, extra_ref, carry_ref, sems, esem, pend_ref):
    t = pl.program_id(0)
    nt = pl.num_programs(0)

    def strip_copy(e, base):
        return pltpu.make_async_copy(stage_ref.at[pl.ds(e * CHUNK, CHUNK), :],
                                     xs_ref.at[e, pl.ds(pl.multiple_of(base, ROW_ALIGN), CHUNK), :], sems.at[e])

    @pl.when(t == 0)
    def _():
        extra_ref[...] = jnp.zeros_like(extra_ref)
        cap = xs_ref.shape[1] - CHUNK
        for e in range(N_EXPERTS):
            pend_ref[e] = 0
            pltpu.make_async_copy(extra_ref, xs_ref.at[e, pl.ds(cap, CHUNK), :], esem).start()
        for e in range(N_EXPERTS):
            pltpu.make_async_copy(extra_ref, xs_ref.at[e, pl.ds(cap, CHUNK), :], esem).wait()

    jrow = lax.broadcasted_iota(I32, (CHUNK, ROUTE_TILE), 0)
    pieces = []
    for e in range(N_EXPERTS):
        _, _, base, _ = _row_window(bs_ref, e, t)
        pieces.append(jnp.where(selpos_ref[e:e + 1, :] == jrow + base, 1.0, 0.0).astype(BF16))
    onehot = jnp.concatenate(pieces, axis=0)

    for e in range(N_EXPERTS):
        @pl.when(pend_ref[e] == 1)
        def _():
            strip_copy(e, 0).wait()
            pend_ref[e] = 0

    stage_ref[:, 0:D] = _dot(onehot, h2_ref[...]).astype(BF16)
    stage_ref[:, D:XS_W] = _dot(onehot, g3_ref[...]).astype(BF16)

    r16 = lax.broadcasted_iota(I32, (ROW_ALIGN, XS_W), 0)
    for e in range(N_EXPERTS):
        start, cnt, base, off = _row_window(bs_ref, e, t)

        @pl.when(cnt > 0)
        def _():
            head = pl.ds(e * CHUNK, ROW_ALIGN)
            stage_ref[head, :] = jnp.where(r16 < off, carry_ref[e], stage_ref[head, :])
            strip_copy(e, base).start()
            pend_ref[e] = 1
            end = start + cnt
            nch = (off + cnt + CHUNK - 1) // CHUNK
            gbase = (end // ROW_ALIGN) * ROW_ALIGN

            @pl.when(nch == 1)
            def _():
                @pl.when(gbase - base < CHUNK)
                def _():
                    src = pl.multiple_of(e * CHUNK + gbase - base, ROW_ALIGN)
                    carry_ref[e] = stage_ref[pl.ds(src, ROW_ALIGN), :]

            def more(cidx, _):
                cb = base + cidx * CHUNK
                oh = jnp.where(selpos_ref[e:e + 1, :] == jrow + cb, 1.0, 0.0).astype(BF16)
                extra_ref[:, 0:D] = _dot(oh, h2_ref[...]).astype(BF16)
                extra_ref[:, D:XS_W] = _dot(oh, g3_ref[...]).astype(BF16)
                cp = pltpu.make_async_copy(extra_ref, xs_ref.at[e, pl.ds(pl.multiple_of(cb, ROW_ALIGN), CHUNK), :], esem)
                cp.start()
                cp.wait()

                @pl.when(jnp.logical_and(cidx == nch - 1, gbase - cb < CHUNK))
                def _():
                    src = pl.multiple_of(gbase - cb, ROW_ALIGN)
                    carry_ref[e] = extra_ref[pl.ds(src, ROW_ALIGN), :]
                return 0

            lax.fori_loop(1, nch, more, 0)

    @pl.when(t == nt - 1)
    def _():
        for e in range(N_EXPERTS):
            @pl.when(pend_ref[e] == 1)
            def _():
                strip_copy(e, 0).wait()
                pend_ref[e] = 0


def _compact(bs, h2, g3, selpos, cap):
    n_tok = h2.shape[0]
    nt = n_tok // ROUTE_TILE
    grid_spec = pltpu.PrefetchScalarGridSpec(
        num_scalar_prefetch=1,
        grid=(nt,),
        in_specs=[pl.BlockSpec((ROUTE_TILE, D), lambda t, bs: (t, 0)),
                  pl.BlockSpec((ROUTE_TILE, LANES), lambda t, bs: (t, 0)),
                  pl.BlockSpec((N_EXPERTS, ROUTE_TILE), lambda t, bs: (0, t))],
        out_specs=pl.BlockSpec(memory_space=pl.ANY),
        scratch_shapes=[pltpu.VMEM((N_EXPERTS * CHUNK, XS_W), BF16),
                        pltpu.VMEM((CHUNK, XS_W), BF16),
                        pltpu.VMEM((N_EXPERTS, ROW_ALIGN, XS_W), BF16),
                        pltpu.SemaphoreType.DMA((N_EXPERTS,)),
                        pltpu.SemaphoreType.DMA(()),
                        pltpu.SMEM((N_EXPERTS,), I32)])
    return pl.pallas_call(
        _compact_kernel,
        grid_spec=grid_spec,
        out_shape=jax.ShapeDtypeStruct((N_EXPERTS, cap + CHUNK, XS_W), BF16),
        compiler_params=_cparams(("arbitrary",)),
        name="compact",
    )(bs, h2, g3, selpos)


def _ffn_kernel(xs_ref, wg_ref, wu_ref, wd_ref, ys_ref):
    e = pl.program_id(0)
    xs = xs_ref[0]
    x = xs[:, 0:D]
    lane = lax.broadcasted_iota(I32, (1, LANES), 1)
    mine = jnp.logical_or(lane == e, jnp.logical_or(lane == e + N_EXPERTS, lane == e + 2 * N_EXPERTS))
    gate = jnp.sum(jnp.where(mine, xs[:, D:XS_W].astype(F32), 0.0), axis=1, keepdims=True)
    a = _dot(x, wg_ref[0])
    hid = (a * _sigmoid(a) * _dot(x, wu_ref[0])).astype(BF16)
    ys_ref[0] = (_dot(hid, wd_ref[0]) * gate).astype(BF16)


def _ffn(xs, w_gate, w_up, w_down, cap):
    tm = min(512, cap)
    assert cap % tm == 0
    wspec = lambda shape: pl.BlockSpec(shape, lambda e, j: (e, 0, 0))
    return pl.pallas_call(
        _ffn_kernel,
        grid=(N_EXPERTS, cap // tm),
        in_specs=[pl.BlockSpec((1, tm, XS_W), lambda e, j: (e, j, 0)),
                  wspec((1, D, FF)), wspec((1, D, FF)), wspec((1, FF, D))],
        out_specs=pl.BlockSpec((1, tm, D), lambda e, j: (e, j, 0)),
        out_shape=jax.ShapeDtypeStruct((N_EXPERTS, cap, D), BF16),
        compiler_params=_cparams(("arbitrary", "arbitrary")),
        name="ffn",
    )(xs, w_gate, w_up, w_down)


def _combine_kernel(bs_ref, x1_ref, post_ref, ada_ref, nw_ref, expand_ref, ys_ref, o_ref,
                    strips_ref, extra_ref, acc_ref, sems, esem, *, cap):
    nts = pl.num_programs(1)
    t = pl.program_id(0) * nts + pl.program_id(1)
    nt = pl.num_programs(0) * nts

    def strip_base(e, tile):
        _, _, base, _ = _row_window(bs_ref, e, tile)
        return jnp.minimum(base, cap - CHUNK)

    def strip_copy(e, tile, slot):
        src = ys_ref.at[e, pl.ds(pl.multiple_of(strip_base(e, tile), ROW_ALIGN), CHUNK), :]
        return pltpu.make_async_copy(src, strips_ref.at[slot, pl.ds(e * CHUNK, CHUNK), :], sems.at[slot, e])

    @pl.when(t == 0)
    def _():
        for e in range(N_EXPERTS):
            strip_copy(e, 0, 0).start()

    @pl.when(t + 1 < nt)
    def _():
        for e in range(N_EXPERTS):
            strip_copy(e, t + 1, (t + 1) % 2).start()

    pos = post_ref[...]
    p_hi = jnp.floor(pos * (1.0 / CHUNK))
    p_lo = pos - p_hi * CHUNK
    hi_x = _dot(p_hi.astype(BF16), expand_ref[...])
    lo_x = _dot(p_lo.astype(BF16), expand_ref[...])
    lane = lax.broadcasted_iota(I32, (1, N_EXPERTS * CHUNK), 1)
    want = lane % CHUNK
    for e in range(N_EXPERTS):
        want = want + jnp.where(lane // CHUNK == e, strip_base(e, t), 0)
    w_hi = (want // CHUNK).astype(F32)
    w_lo = (want % CHUNK).astype(F32)
    onehot = jnp.where(jnp.logical_and(hi_x == w_hi, lo_x == w_lo), 1.0, 0.0).astype(BF16)

    slot = t % 2
    for e in range(N_EXPERTS):
        strip_copy(e, t, slot).wait()
    acc_ref[...] = _dot(onehot, strips_ref[slot])

    for e in range(N_EXPERTS):
        start, cnt, base, off = _row_window(bs_ref, e, t)
        nch = (off + cnt + CHUNK - 1) // CHUNK

        def more(cidx, _):
            cb = jnp.minimum(base + cidx * CHUNK, cap - CHUNK)
            cp = pltpu.make_async_copy(ys_ref.at[e, pl.ds(pl.multiple_of(cb, ROW_ALIGN), CHUNK), :], extra_ref, esem)
            cp.start()
            tgt = (lax.broadcasted_iota(I32, (1, CHUNK), 1) + cb).astype(F32)
            prev_end = (jnp.minimum(base + (cidx - 1) * CHUNK, cap - CHUNK) + CHUNK).astype(F32)
            col = pos[:, e:e + 1]
            oh = jnp.where(jnp.logical_and(col == tgt, col >= prev_end), 1.0, 0.0).astype(BF16)
            cp.wait()
            acc_ref[...] += _dot(oh, extra_ref[...])
            return 0

        lax.fori_loop(1, nch, more, 0)

    y = acc_ref[...]
    o_ref[0] = x1_ref[0] + _rms(y, ada_ref[0, 5:6, :] * nw_ref[...])


def _combine(bs, x1, post, ada, norm_w, expand, ys, cap):
    B, S, _ = x1.shape
    nts = S // ROUTE_TILE
    grid_spec = pltpu.PrefetchScalarGridSpec(
        num_scalar_prefetch=1,
        grid=(B, nts),
        in_specs=[pl.BlockSpec((1, ROUTE_TILE, D), lambda b, i, bs: (b, i, 0)),
                  pl.BlockSpec((ROUTE_TILE, LANES), lambda b, i, bs: (b * nts + i, 0)),
                  pl.BlockSpec((1, 6, D), lambda b, i, bs: (b, 0, 0)),
                  pl.BlockSpec((1, D), lambda b, i, bs: (0, 0)),
                  pl.BlockSpec((LANES, N_EXPERTS * CHUNK), lambda b, i, bs: (0, 0)),
                  pl.BlockSpec(memory_space=pl.ANY)],
        out_specs=pl.BlockSpec((1, ROUTE_TILE, D), lambda b, i, bs: (b, i, 0)),
        scratch_shapes=[pltpu.VMEM((2, N_EXPERTS * CHUNK, D), BF16),
                        pltpu.VMEM((CHUNK, D), BF16),
                        pltpu.VMEM((ROUTE_TILE, D), F32),
                        pltpu.SemaphoreType.DMA((2, N_EXPERTS)),
                        pltpu.SemaphoreType.DMA(())])
    return pl.pallas_call(
        functools.partial(_combine_kernel, cap=cap),
        grid_spec=grid_spec,
        out_shape=jax.ShapeDtypeStruct((B, S, D), F32),
        compiler_params=_cparams(("arbitrary", "arbitrary")),
        name="combine",
    )(bs, x1, post, ada, norm_w.reshape(1, D), expand, ys)


def _prepare_weights(w_in, w_branch_attn, w_branch_gla, w_out, w_router, w_gate_e, w_up_e, w_down_e):
    aq, ak, av, gq, gk, gv, gr, glf, glb, ga, gg = jnp.split(
        w_in, np.cumsum([512, 128, 128, 256, 256, 512, 512, 16, 16, 1024]).tolist(), axis=1)
    pad = jnp.zeros((D, NP - C_GL - 2 * GLA_RANK), F32)
    w_in_p = jnp.concatenate([aq, gv, ga, gg, gr, ak, av, gq, gk, glf, glb, pad], axis=1).astype(BF16)
    wr = jnp.zeros((D, LANES), F32).at[:, 0:N_EXPERTS].set(w_router)
    wr_hi = wr.astype(BF16)
    wr_cat = jnp.concatenate([wr_hi, (wr - wr_hi.astype(F32)).astype(BF16)], axis=1)
    expand = np.zeros((LANES, N_EXPERTS * CHUNK), np.float32)
    for e in range(N_EXPERTS):
        expand[e, e * CHUNK:(e + 1) * CHUNK] = 1.0
    return dict(w_in_p=w_in_p, w_ba=w_branch_attn.astype(BF16), w_bg=w_branch_gla.astype(BF16),
                w_out=w_out.astype(BF16), wr=wr_cat, w_gate=w_gate_e.astype(BF16),
                w_up=w_up_e.astype(BF16), w_down=w_down_e.astype(BF16), expand=jnp.asarray(expand, BF16),
                bias=_attn_bias_table())


def _layer(x, ada, w, p):
    B, S, _ = x.shape
    n_tok = B * S
    cap = CAPACITY_FACTOR * n_tok // N_EXPERTS
    assert n_tok % ROUTE_TILE == 0 and S % ROUTE_TILE == 0 and cap % CHUNK == 0
    proj = _inproj(x, ada, p['norm_pre_mix'], w['w_in_p'])
    attn = _attention(proj, p['attn_sink'], w['bias'])
    o_g = _gla(proj, p['gla_wa2_fwd'], p['gla_ba_fwd'], p['gla_wa2_bwd'], p['gla_ba_bwd'], p['gla_norm'])
    x1, h2, g3, aff_t = _post(x, attn, o_g, proj, ada, w['w_ba'], w['w_bg'], w['w_out'],
                              p['norm_post_mix'], p['norm_pre_ffn'], w['wr'])
    selpos, post, bs = _select(aff_t, cap)
    xs = _compact(bs, h2.reshape(n_tok, D), g3.reshape(n_tok, LANES), selpos, cap)
    ys = _ffn(xs, w['w_gate'], w['w_up'], w['w_down'], cap)
    return _combine(bs, x1, post, ada, p['norm_post_ffn'], w['expand'], ys, cap)


def kernel(x_prompt, x_sample, c_prompt, c_sample, w_ada, b_ada, norm_pre_mix, norm_post_mix, w_in, attn_sink, gla_wa2_fwd, gla_ba_fwd, gla_wa2_bwd, gla_ba_bwd, gla_norm, w_branch_attn, w_branch_gla, w_out, norm_pre_ffn, norm_post_ffn, w_router, w_gate_e, w_up_e, w_down_e):
    assert w_ada.shape[0] == 1, "one layer"
    p = dict(norm_pre_mix=norm_pre_mix[0], norm_post_mix=norm_post_mix[0], attn_sink=attn_sink[0],
             gla_wa2_fwd=gla_wa2_fwd[0], gla_ba_fwd=gla_ba_fwd[0], gla_wa2_bwd=gla_wa2_bwd[0],
             gla_ba_bwd=gla_ba_bwd[0], gla_norm=gla_norm[0], norm_pre_ffn=norm_pre_ffn[0],
             norm_post_ffn=norm_post_ffn[0])
    w = _prepare_weights(w_in[0], w_branch_attn[0], w_branch_gla[0], w_out[0], w_router[0],
                         w_gate_e[0], w_up_e[0], w_down_e[0])
    bp, bs_ = c_prompt.shape[0], c_sample.shape[0]
    rows = -(-(bp + bs_) // 8) * 8
    c_all = jnp.concatenate([c_prompt, c_sample, jnp.zeros((rows - bp - bs_, D), F32)], axis=0)
    ada = _ada(c_all, w_ada[0], b_ada[0]).reshape(rows, 6, D)
    y_prompt = _layer(x_prompt, ada[0:bp], w, p)
    y_sample = _layer(x_sample, ada[bp:bp + bs_], w, p)
    return (y_prompt, y_sample)
```

```python
import functools

import numpy as np
import jax
import jax.numpy as jnp
from jax import lax
from jax.experimental import pallas as pl
from jax.experimental.pallas import tpu as pltpu

F32, BF16, I32 = jnp.float32, jnp.bfloat16, jnp.int32

D = 1024
N_Q_HEADS, N_KV_HEADS, HEAD_DIM = 8, 2, 64
ATTN_BLOCK = 128
GLA_HEADS, GLA_K, GLA_V = 4, 64, 128
GLA_RANK = 16
GLA_NORMALIZER = 16.0
N_EXPERTS, FF = 16, 1024
CAPACITY_FACTOR = 2
EPS = 1e-6
NEG = -1e30

C_AQ, C_GV, C_GA, C_GG, C_GR, C_KV, C_GQ, C_GK, C_GL = 0, 512, 1024, 2048, 3072, 3584, 3840, 4096, 4352
NP = 4480

LANES = 128
ROW_ALIGN = 16
CHUNK = 64
ROUTE_TILE = 256
XS_W = D + LANES
VMEM_LIMIT = 56 * 1024 * 1024


def _cparams(sem):
    return pltpu.CompilerParams(dimension_semantics=sem, vmem_limit_bytes=VMEM_LIMIT)


def _dot(a, b):
    return jnp.dot(a, b, preferred_element_type=F32)


def _dot_nt(a, b):
    return lax.dot_general(a, b, (((1,), (1,)), ((), ())), preferred_element_type=F32)


def _dot_tn(a, b):
    return lax.dot_general(a, b, (((0,), (0,)), ((), ())), preferred_element_type=F32)


def _split2(x):
    hi = x.astype(BF16)
    lo = (x - hi.astype(F32)).astype(BF16)
    return hi, lo


def _split3(x):
    hi = x.astype(BF16)
    r = x - hi.astype(F32)
    mid = r.astype(BF16)
    lo = (r - mid.astype(F32)).astype(BF16)
    return hi, mid, lo


def _sigmoid(x):
    return 0.5 * jnp.tanh(0.5 * x) + 0.5


def _rms(x, w):
    return x * lax.rsqrt(jnp.mean(x * x, axis=-1, keepdims=True) + EPS) * w


def _ada_kernel(c_ref, w_ref, b_ref, o_ref):
    c = c_ref[...]
    s = c * jax.nn.sigmoid(c)
    s_hi, s_lo = _split2(s)
    w_hi, w_lo = _split2(w_ref[...])
    o_ref[...] = _dot(s_hi, w_hi) + _dot(s_lo, w_hi) + _dot(s_hi, w_lo) + b_ref[...]


def _ada(c, w_ada, b_ada):
    rows = c.shape[0]
    tn = 1024
    return pl.pallas_call(
        _ada_kernel,
        grid=(6 * D // tn,),
        in_specs=[pl.BlockSpec((rows, D), lambda j: (0, 0)),
                  pl.BlockSpec((D, tn), lambda j: (0, j)),
                  pl.BlockSpec((1, tn), lambda j: (0, j))],
        out_specs=pl.BlockSpec((rows, tn), lambda j: (0, j)),
        out_shape=jax.ShapeDtypeStruct((rows, 6 * D), F32),
        compiler_params=_cparams(("arbitrary",)),
        name="ada",
    )(c, w_ada, b_ada.reshape(1, 6 * D))


def _inproj_kernel(x_ref, ada_ref, nw_ref, w_ref, o_ref):
    x = x_ref[0]
    h = (_rms(x, nw_ref[...] * (1.0 + ada_ref[0, 1:2, :])) + ada_ref[0, 0:1, :]).astype(BF16)
    step = 1024
    for c0 in range(0, NP, step):
        c1 = min(c0 + step, NP)
        o_ref[0, :, c0:c1] = _dot(h, w_ref[:, c0:c1]).astype(BF16)


def _inproj(x, ada, norm_w, w_in_p):
    B, S, _ = x.shape
    tm = 512
    return pl.pallas_call(
        _inproj_kernel,
        grid=(B, S // tm),
        in_specs=[pl.BlockSpec((1, tm, D), lambda b, i: (b, i, 0)),
                  pl.BlockSpec((1, 6, D), lambda b, i: (b, 0, 0)),
                  pl.BlockSpec((1, D), lambda b, i: (0, 0)),
                  pl.BlockSpec((D, NP), lambda b, i: (0, 0))],
        out_specs=pl.BlockSpec((1, tm, NP), lambda b, i: (b, i, 0)),
        out_shape=jax.ShapeDtypeStruct((B, S, NP), BF16),
        compiler_params=_cparams(("arbitrary", "arbitrary")),
        name="inproj",
    )(x, ada, norm_w.reshape(1, D), w_in_p)


ATTN_TQ = 512
ATTN_QB = ATTN_TQ // ATTN_BLOCK


_ATTN_STACK = (0, 2, 1, 3)


def _attn_bias_table():
    L = ATTN_BLOCK
    group = N_Q_HEADS // N_KV_HEADS
    q = np.arange(L)[:, None]
    k = np.arange(3 * L)[None, :]
    dist = np.abs(q - k + L).astype(np.float32)
    slopes = np.asarray([2.0 ** (-8.0 * (h + 1) / N_Q_HEADS) for h in range(N_Q_HEADS)], np.float32)
    base = -slopes[:, None, None] * dist[None]
    inside = (dist <= L)[None]
    variants = []
    for ok_k in (np.ones_like(k, bool), k >= L, k < 2 * L):
        per_head = np.where(inside & ok_k[None], base, np.float32(NEG))
        variants.append(np.stack([np.concatenate([per_head[hkv * group + g] for g in _ATTN_STACK], axis=0)
                                  for hkv in range(N_KV_HEADS)]))
    return jnp.asarray(np.stack(variants), F32)


def _attn_kernel(sink_ref, q_ref, kvm_ref, kvp_ref, kvn_ref, bias_ref, o_ref):
    L = ATTN_BLOCK
    i = pl.program_id(1)
    last = pl.num_programs(1) - 1
    kv = jnp.concatenate([kvp_ref[0], kvm_ref[0], kvn_ref[0]], axis=0)
    k = kv[:, :LANES] * jnp.asarray(HEAD_DIM ** -0.5, BF16)
    v = kv[:, LANES:]
    lo = lax.broadcasted_iota(I32, (1, LANES), 1) < HEAD_DIM
    zero = jnp.zeros_like(k)

    def halves(t):
        sw = jnp.concatenate([t[:, HEAD_DIM:], t[:, :HEAD_DIM]], axis=1)
        return ((jnp.where(lo, t, zero), jnp.where(lo, zero, sw)),
                (jnp.where(lo, sw, zero), jnp.where(lo, zero, t)))

    kh, vh = halves(k), halves(v)
    group = N_Q_HEADS // N_KV_HEADS
    rowblk = lax.broadcasted_iota(I32, (group * L, 1), 0) // L
    for j in range(ATTN_QB):
        first = jnp.logical_and(i == 0, j == 0)
        final = jnp.logical_and(i == last, j == ATTN_QB - 1)
        var = jnp.where(first, 1, jnp.where(final, 2, 0))
        r0 = j * L
        for hkv in range(N_KV_HEADS):
            c0 = hkv * 2 * LANES
            qs = jnp.concatenate([q_ref[0, r0:r0 + L, c0:c0 + LANES],
                                  q_ref[0, r0:r0 + L, c0 + LANES:c0 + 2 * LANES]], axis=0)
            s = jnp.concatenate([_dot_nt(qs, kh[hkv][0][r0:r0 + 3 * L]),
                                 _dot_nt(qs, kh[hkv][1][r0:r0 + 3 * L])], axis=0) + bias_ref[var, hkv]
            snk = jnp.zeros((group * L, 1), F32)
            for b, g in enumerate(_ATTN_STACK):
                snk = jnp.where(rowblk == b, sink_ref[hkv * group + g], snk)
            mx = jnp.maximum(jnp.max(s, axis=-1, keepdims=True), snk)
            p = jnp.exp(s - mx)
            rden = 1.0 / (jnp.sum(p, axis=-1, keepdims=True) + jnp.exp(snk - mx))
            pb = p.astype(BF16)
            o = (_dot(pb[0:2 * L], vh[hkv][0][r0:r0 + 3 * L]) * rden[0:2 * L]
                 + _dot(pb[2 * L:4 * L], vh[hkv][1][r0:r0 + 3 * L]) * rden[2 * L:4 * L])
            o_ref[0, r0:r0 + L, c0:c0 + LANES] = o[0:L].astype(BF16)
            o_ref[0, r0:r0 + L, c0 + LANES:c0 + 2 * LANES] = o[L:2 * L].astype(BF16)


def _attention(proj, sink, bias):
    B, S, _ = proj.shape
    L = ATTN_BLOCK
    nb = S // L
    assert S % ATTN_TQ == 0 and nb >= 2
    kvc = C_KV // 256
    return pl.pallas_call(
        _attn_kernel,
        grid=(B, S // ATTN_TQ),
        in_specs=[pl.BlockSpec(memory_space=pltpu.SMEM),
                  pl.BlockSpec((1, ATTN_TQ, 512), lambda b, i: (b, i, C_AQ // 512)),
                  pl.BlockSpec((1, ATTN_TQ, 256), lambda b, i: (b, i, kvc)),
                  pl.BlockSpec((1, L, 256), lambda b, i: (b, jnp.maximum(i * ATTN_QB - 1, 0), kvc)),
                  pl.BlockSpec((1, L, 256), lambda b, i: (b, jnp.minimum(i * ATTN_QB + ATTN_QB, nb - 1), kvc)),
                  pl.BlockSpec((3, N_KV_HEADS, (N_Q_HEADS // N_KV_HEADS) * L, 3 * L), lambda b, i: (0, 0, 0, 0))],
        out_specs=pl.BlockSpec((1, ATTN_TQ, 512), lambda b, i: (b, i, 0)),
        out_shape=jax.ShapeDtypeStruct((B, S, 512), BF16),
        compiler_params=_cparams(("arbitrary", "arbitrary")),
        name="attn",
    )(sink, proj, proj, proj, proj, bias)


GLA_TS = 512
GLA_C = 128
GLA_NC = GLA_TS // GLA_C


def _gla_tile(q_ref, k_ref, v_ref, gl_ref, w_ref, ba_ref, st_ref, rev):
    C = GLA_C
    x = _dot(gl_ref[0], w_ref[...]) + ba_ref[...]
    la = (jnp.minimum(x, 0.0) - jnp.log(1.0 + jnp.exp(-jnp.abs(x)))) * (1.0 / GLA_NORMALIZER)
    row = lax.broadcasted_iota(I32, (C, C), 0)
    col = lax.broadcasted_iota(I32, (C, C), 1)
    if rev:
        tri = jnp.where(col >= row, 1.0, 0.0).astype(BF16)
        keep = col > row
    else:
        tri = jnp.where(col <= row, 1.0, 0.0).astype(BF16)
        keep = col <= row
    hi, mid, lo3 = _split3(la)
    bcs, halves, ehs = [], [], []
    for c in range(GLA_NC):
        rs = slice(c * C, (c + 1) * C)
        bc = _dot(tri, hi[rs]) + _dot(tri, mid[rs]) + _dot(tri, lo3[rs])
        half = 0.5 * (bc[0:1] if rev else bc[C - 1:C])
        bcs.append(bc)
        halves.append(jnp.broadcast_to(half, bc.shape))
        ehs.append(jnp.exp(half))
    bc = jnp.concatenate(bcs, axis=0)
    half = jnp.concatenate(halves, axis=0)
    eh = jnp.concatenate([jnp.broadcast_to(e, (C, e.shape[1])) for e in ehs], axis=0)
    qe = q_ref[0].astype(F32) * (GLA_K ** -0.5) * jnp.exp(bc - half)
    ke = k_ref[0].astype(F32) * jnp.exp(half - bc)
    qt, kt = qe.astype(BF16), ke.astype(BF16)
    qh = (qe * eh).astype(BF16)
    kb = (ke * eh).astype(BF16)
    lo = lax.broadcasted_iota(I32, (1, LANES), 1) < GLA_K
    zero = jnp.zeros((GLA_TS, LANES), BF16)
    order = range(GLA_NC - 1, -1, -1) if rev else range(GLA_NC)
    outs = [[None] * GLA_HEADS for _ in range(GLA_NC)]
    for m in range(GLA_HEADS // 2):
        sl = slice(m * LANES, (m + 1) * LANES)
        ktm = (jnp.where(lo, kt[:, sl], zero), jnp.where(lo, zero, kt[:, sl]))
        qhm = (jnp.where(lo, qh[:, sl], zero), jnp.where(lo, zero, qh[:, sl]))
        st = st_ref[m]
        for c in order:
            rs = slice(c * C, (c + 1) * C)
            stb = st.astype(BF16)
            upd = []
            for par in range(2):
                h = 2 * m + par
                a = jnp.where(keep, _dot_nt(qt[rs, sl], ktm[par][rs]), 0.0).astype(BF16)
                vh = v_ref[0, rs, h * GLA_V:(h + 1) * GLA_V]
                outs[c][h] = _dot(a, vh) + _dot_nt(qhm[par][rs], stb)
                upd.append(_dot_tn(vh, kb[rs, sl]))
            dec = ehs[c][:, sl] * ehs[c][:, sl]
            st = st * dec + jnp.where(lo, upd[0], upd[1])
        st_ref[m] = st
    return outs


def _gla_fwd_kernel(q_ref, k_ref, v_ref, gl_ref, w_ref, ba_ref, o_ref, st_ref):
    @pl.when(pl.program_id(1) == 0)
    def _():
        st_ref[...] = jnp.zeros_like(st_ref)

    outs = _gla_tile(q_ref, k_ref, v_ref, gl_ref, w_ref, ba_ref, st_ref, rev=False)
    for c in range(GLA_NC):
        for h in range(GLA_HEADS):
            o_ref[0, c * GLA_C:(c + 1) * GLA_C, h * GLA_V:(h + 1) * GLA_V] = outs[c][h].astype(BF16)


def _gla_bwd_kernel(q_ref, k_ref, v_ref, gl_ref, w_ref, ba_ref, of_ref, gr_ref, nw_ref, o_ref, st_ref):
    @pl.when(pl.program_id(1) == 0)
    def _():
        st_ref[...] = jnp.zeros_like(st_ref)

    outs = _gla_tile(q_ref, k_ref, v_ref, gl_ref, w_ref, ba_ref, st_ref, rev=True)
    for h in range(GLA_HEADS):
        hs = slice(h * GLA_V, (h + 1) * GLA_V)
        o = jnp.concatenate([outs[c][h] for c in range(GLA_NC)], axis=0) + of_ref[0, :, hs].astype(F32)
        g = gr_ref[0, :, hs].astype(F32)
        o_ref[0, :, hs] = (_rms(o, nw_ref[...]) * (g * _sigmoid(g))).astype(BF16)


def _gla(proj, wa2_f, ba_f, wa2_b, ba_b, gla_norm):
    B, S, _ = proj.shape
    nt = S // GLA_TS
    assert S % GLA_TS == 0
    kw = GLA_HEADS * GLA_K
    wf = jnp.zeros((LANES, kw), F32).at[0:GLA_RANK].set(wa2_f).astype(BF16)
    wb = jnp.zeros((LANES, kw), F32).at[GLA_RANK:2 * GLA_RANK].set(wa2_b).astype(BF16)

    def specs(tile):
        return [pl.BlockSpec((1, GLA_TS, 256), lambda b, i: (b, tile(i), C_GQ // 256)),
                pl.BlockSpec((1, GLA_TS, 256), lambda b, i: (b, tile(i), C_GK // 256)),
                pl.BlockSpec((1, GLA_TS, 512), lambda b, i: (b, tile(i), C_GV // 512)),
                pl.BlockSpec((1, GLA_TS, LANES), lambda b, i: (b, tile(i), C_GL // LANES)),
                pl.BlockSpec((LANES, kw), lambda b, i: (0, 0)),
                pl.BlockSpec((1, kw), lambda b, i: (0, 0))]

    scratch = [pltpu.VMEM((GLA_HEADS // 2, GLA_V, LANES), F32)]
    fwd_tile = lambda i: i
    o_f = pl.pallas_call(
        _gla_fwd_kernel,
        grid=(B, nt),
        in_specs=specs(fwd_tile),
        out_specs=pl.BlockSpec((1, GLA_TS, 512), lambda b, i: (b, i, 0)),
        out_shape=jax.ShapeDtypeStruct((B, S, 512), BF16),
        scratch_shapes=scratch,
        compiler_params=_cparams(("arbitrary", "arbitrary")),
        name="gla_fwd",
    )(proj, proj, proj, proj, wf, ba_f.reshape(1, kw))
    bwd_tile = lambda i: nt - 1 - i
    return pl.pallas_call(
        _gla_bwd_kernel,
        grid=(B, nt),
        in_specs=specs(bwd_tile) + [
            pl.BlockSpec((1, GLA_TS, 512), lambda b, i: (b, bwd_tile(i), 0)),
            pl.BlockSpec((1, GLA_TS, 512), lambda b, i: (b, bwd_tile(i), C_GR // 512)),
            pl.BlockSpec((1, GLA_V), lambda b, i: (0, 0))],
        out_specs=pl.BlockSpec((1, GLA_TS, 512), lambda b, i: (b, bwd_tile(i), 0)),
        out_shape=jax.ShapeDtypeStruct((B, S, 512), BF16),
        scratch_shapes=scratch,
        compiler_params=_cparams(("arbitrary", "arbitrary")),
        name="gla_bwd",
    )(proj, proj, proj, proj, wb, ba_b.reshape(1, kw), o_f, proj, gla_norm.reshape(1, GLA_V))


POST_SUB = 256

def _post_kernel(x_ref, attn_ref, og_ref, ga_ref, gg_ref, ada_ref, wba_ref, wbg_ref, wo_ref, npost_ref,
                 npre_ref, wr_ref, x1_ref, h2_ref, g3_ref, afft_ref):
    w_post = ada_ref[0, 2:3, :] * npost_ref[...]
    w_pre = (1.0 + ada_ref[0, 4:5, :]) * npre_ref[...]
    shift = ada_ref[0, 3:4, :]
    for r0 in range(0, x_ref.shape[1], POST_SUB):
        rs = slice(r0, r0 + POST_SUB)
        ga = ga_ref[0, rs, :].astype(F32)
        gg = gg_ref[0, rs, :].astype(F32)
        merged = (_sigmoid(ga) * _dot(attn_ref[0, rs, :], wba_ref[...])
                  + _sigmoid(gg) * _dot(og_ref[0, rs, :], wbg_ref[...]))
        mix = _dot(merged.astype(BF16), wo_ref[...])
        x1 = x_ref[0, rs, :] + _rms(mix, w_post)
        x1_ref[0, rs, :] = x1
        h2 = _rms(x1, w_pre) + shift
        h2_ref[0, rs, :] = h2.astype(BF16)
        h_hi, h_lo = _split2(h2)
        r = _dot(h_hi, wr_ref[...]) + _dot(h_lo, wr_ref[...])
        logits = r[:, 0:LANES] + r[:, LANES:2 * LANES]
        lane = lax.broadcasted_iota(I32, logits.shape, 1)
        logits = jnp.where(lane < N_EXPERTS, logits, NEG)
        e = jnp.exp(logits - jnp.max(logits, axis=-1, keepdims=True))
        aff = e / jnp.sum(e, axis=-1, keepdims=True)
        a_hi, a_mid, a_lo = _split3(aff)
        g3 = (a_hi.astype(F32) + pltpu.roll(a_mid.astype(F32), N_EXPERTS, 1)
              + pltpu.roll(a_lo.astype(F32), 2 * N_EXPERTS, 1))
        g3_ref[0, rs, :] = g3.astype(BF16)
        afft_ref[:, rs] = aff.T[0:N_EXPERTS, :]


def _post(x, attn, o_g, proj, ada, w_ba, w_bg, w_out, n_post, n_pre, wr):
    B, S, _ = x.shape
    tm = 512
    nt = S // tm
    const = lambda b, i: (0, 0)
    return pl.pallas_call(
        _post_kernel,
        grid=(B, nt),
        in_specs=[pl.BlockSpec((1, tm, D), lambda b, i: (b, i, 0)),
                  pl.BlockSpec((1, tm, 512), lambda b, i: (b, i, 0)),
                  pl.BlockSpec((1, tm, 512), lambda b, i: (b, i, 0)),
                  pl.BlockSpec((1, tm, D), lambda b, i: (b, i, C_GA // D)),
                  pl.BlockSpec((1, tm, D), lambda b, i: (b, i, C_GG // D)),
                  pl.BlockSpec((1, 6, D), lambda b, i: (b, 0, 0)),
                  pl.BlockSpec((512, D), const),
                  pl.BlockSpec((512, D), const),
                  pl.BlockSpec((D, D), const),
                  pl.BlockSpec((1, D), const),
                  pl.BlockSpec((1, D), const),
                  pl.BlockSpec((D, 2 * LANES), const)],
        out_specs=[pl.BlockSpec((1, tm, D), lambda b, i: (b, i, 0)),
                   pl.BlockSpec((1, tm, D), lambda b, i: (b, i, 0)),
                   pl.BlockSpec((1, tm, LANES), lambda b, i: (b, i, 0)),
                   pl.BlockSpec((N_EXPERTS, tm), lambda b, i: (0, b * nt + i))],
        out_shape=[jax.ShapeDtypeStruct((B, S, D), F32),
                   jax.ShapeDtypeStruct((B, S, D), BF16),
                   jax.ShapeDtypeStruct((B, S, LANES), BF16),
                   jax.ShapeDtypeStruct((N_EXPERTS, B * S), F32)],
        compiler_params=_cparams(("arbitrary", "arbitrary")),
        name="post",
    )(x, attn, o_g, proj, proj, ada, w_ba, w_bg, w_out, n_post.reshape(1, D), n_pre.reshape(1, D), wr)


def _select_kernel(aff_ref, selpos_ref, post_ref, bs_ref, *, cap, n_tok):
    nblk = n_tok // LANES

    def count(mask):
        return jnp.sum(jnp.where(mask, 1.0, 0.0), axis=1, keepdims=True).astype(I32)

    def bit_body(it, thr):
        cand = thr | jnp.left_shift(jnp.int32(1), 30 - it)
        bits = pltpu.bitcast(aff_ref[...], I32)
        return jnp.where(count(bits >= cand) >= cap, cand, thr)

    thr = lax.fori_loop(0, 31, bit_body, jnp.zeros((N_EXPERTS, 1), I32))
    need = cap - count(pltpu.bitcast(aff_ref[...], I32) > thr)
    r = lax.broadcasted_iota(I32, (LANES, LANES), 0)
    c = lax.broadcasted_iota(I32, (LANES, LANES), 1)
    scan_mat = jnp.concatenate([jnp.where(r <= c, 1.0, 0.0), jnp.ones((LANES, LANES), F32)], axis=1).astype(BF16)
    blk_lane = lax.broadcasted_iota(I32, bs_ref.shape, 1)
    filler = jnp.full((LANES - N_EXPERTS, LANES), -1.0, F32)
    reps = bs_ref.shape[1] // LANES

    def taken_before(c_gt, c_eq):
        return jnp.concatenate([c_gt + jnp.minimum(c_eq, need)] * reps, axis=1)

    def blk_body(j, carry):
        c_gt, c_eq, bs = carry
        off = pl.multiple_of(j * LANES, LANES)
        b = pltpu.bitcast(aff_ref[:, pl.ds(off, LANES)], I32)
        gt = b > thr
        eq = b == thr
        both = jnp.concatenate([jnp.where(gt, 1.0, 0.0), jnp.where(eq, 1.0, 0.0)], axis=0).astype(BF16)
        scan = _dot(both, scan_mat).astype(I32)
        ex_gt = c_gt + scan[0:N_EXPERTS, 0:LANES] - jnp.where(gt, 1, 0)
        ex_eq = c_eq + scan[N_EXPERTS:, 0:LANES] - jnp.where(eq, 1, 0)
        sel = jnp.logical_or(gt, jnp.logical_and(eq, ex_eq < need))
        pos = ex_gt + jnp.minimum(ex_eq, need)
        sp = jnp.where(sel, pos, -1)
        selpos_ref[:, pl.ds(off, LANES)] = sp
        post_ref[pl.ds(off, LANES), :] = jnp.concatenate([sp.astype(F32), filler], axis=0).T
        bs = jnp.where(blk_lane == j, taken_before(c_gt, c_eq), bs)
        return c_gt + scan[0:N_EXPERTS, LANES:], c_eq + scan[N_EXPERTS:, LANES:], bs

    z = jnp.zeros((N_EXPERTS, LANES), I32)
    c_gt, c_eq, bs = lax.fori_loop(0, nblk, blk_body, (z, z, jnp.zeros(bs_ref.shape, I32)), unroll=2)
    bs_ref[...] = jnp.where(blk_lane >= nblk, taken_before(c_gt, c_eq), bs)


def _select(aff_t, cap):
    n_tok = aff_t.shape[1]
    nbp = ((n_tok // LANES + 1 + LANES - 1) // LANES) * LANES
    return pl.pallas_call(
        functools.partial(_select_kernel, cap=cap, n_tok=n_tok),
        out_shape=[jax.ShapeDtypeStruct((N_EXPERTS, n_tok), I32),
                   jax.ShapeDtypeStruct((n_tok, LANES), F32),
                   jax.ShapeDtypeStruct((N_EXPERTS, nbp), I32)],
        compiler_params=pltpu.CompilerParams(vmem_limit_bytes=VMEM_LIMIT),
        name="select",
    )(aff_t)


def _row_window(bs_ref, e, tile):
    per = ROUTE_TILE // LANES
    start = bs_ref[e, tile * per]
    cnt = bs_ref[e, tile * per + per] - start
    base = (start // ROW_ALIGN) * ROW_ALIGN
    return start, cnt, base, start - base


def _compact_kernel(bs_ref, h2_ref, g3_ref, selpos_ref, xs_ref, stage_ref, over_ref, carry_ref, sems, osems, pend_ref):
    t = pl.program_id(0)
    nt = pl.num_programs(0)
    par = t % 2
    cap = xs_ref.shape[1] - CHUNK

    def strip_copy(slot, e, base):
        return pltpu.make_async_copy(stage_ref.at[slot, pl.ds(e * CHUNK, CHUNK), :],
                                     xs_ref.at[e, pl.ds(pl.multiple_of(base, ROW_ALIGN), CHUNK), :], sems.at[slot, e])

    def over_copy(e, base):
        return pltpu.make_async_copy(over_ref.at[e], xs_ref.at[e, pl.ds(pl.multiple_of(base, ROW_ALIGN), CHUNK), :],
                                     osems.at[e])

    @pl.when(t == 0)
    def _():
        over_ref[0] = jnp.zeros((CHUNK, XS_W), BF16)
        carry_ref[...] = jnp.zeros_like(carry_ref)
        for e in range(N_EXPERTS):
            pend_ref[e] = 0
            pltpu.make_async_copy(over_ref.at[0], xs_ref.at[e, pl.ds(cap, CHUNK), :], osems.at[e]).start()
        for e in range(N_EXPERTS):
            pltpu.make_async_copy(over_ref.at[0], xs_ref.at[e, pl.ds(cap, CHUNK), :], osems.at[e]).wait()

    jrow = lax.broadcasted_iota(I32, (CHUNK, ROUTE_TILE), 0)
    pieces = []
    for e in range(N_EXPERTS):
        _, _, base, _ = _row_window(bs_ref, e, t)
        pieces.append(jnp.where(selpos_ref[e:e + 1, :] - base == jrow, 1.0, 0.0).astype(BF16))
    onehot = jnp.concatenate(pieces, axis=0)
    stage_ref[par, :, 0:D] = _dot(onehot, h2_ref[...]).astype(BF16)
    stage_ref[par, :, D:XS_W] = _dot(onehot, g3_ref[...]).astype(BF16)

    r16 = lax.broadcasted_iota(I32, (ROW_ALIGN, XS_W), 0)
    for e in range(N_EXPERTS):
        start, cnt, base, off = _row_window(bs_ref, e, t)
        head = pl.ds(e * CHUNK, ROW_ALIGN)
        stage_ref[par, head, :] = jnp.where(r16 < off, carry_ref[e], stage_ref[par, head, :])

        @pl.when(t > 0)
        def _():
            strip_copy(1 - par, e, 0).wait()

        @pl.when(pend_ref[e] == 1)
        def _():
            over_copy(e, 0).wait()
            pend_ref[e] = 0

        strip_copy(par, e, base).start()
        end = start + cnt
        nch = (off + cnt + CHUNK - 1) // CHUNK
        gbase = (end // ROW_ALIGN) * ROW_ALIGN
        src = pl.multiple_of(e * CHUNK + jnp.minimum(gbase - base, CHUNK - ROW_ALIGN), ROW_ALIGN)
        carry_ref[e] = stage_ref[par, pl.ds(src, ROW_ALIGN), :]

        def more(cidx, _):
            cb = base + cidx * CHUNK

            @pl.when(pend_ref[e] == 1)
            def _():
                over_copy(e, 0).wait()

            oh = jnp.where(selpos_ref[e:e + 1, :] - cb == jrow, 1.0, 0.0).astype(BF16)
            over_ref[e, :, 0:D] = _dot(oh, h2_ref[...]).astype(BF16)
            over_ref[e, :, D:XS_W] = _dot(oh, g3_ref[...]).astype(BF16)
            over_copy(e, cb).start()
            pend_ref[e] = 1

            @pl.when(jnp.logical_and(cidx == nch - 1, gbase - cb < CHUNK))
            def _():
                carry_ref[e] = over_ref[e, pl.ds(pl.multiple_of(gbase - cb, ROW_ALIGN), ROW_ALIGN), :]
            return 0

        lax.fori_loop(1, nch, more, 0)

    @pl.when(t == nt - 1)
    def _():
        for e in range(N_EXPERTS):
            strip_copy(par, e, 0).wait()

            @pl.when(pend_ref[e] == 1)
            def _():
                over_copy(e, 0).wait()
                pend_ref[e] = 0


def _compact(bs, h2, g3, selpos, cap):
    n_tok = h2.shape[0]
    nt = n_tok // ROUTE_TILE
    grid_spec = pltpu.PrefetchScalarGridSpec(
        num_scalar_prefetch=1,
        grid=(nt,),
        in_specs=[pl.BlockSpec((ROUTE_TILE, D), lambda t, bs: (t, 0)),
                  pl.BlockSpec((ROUTE_TILE, LANES), lambda t, bs: (t, 0)),
                  pl.BlockSpec((N_EXPERTS, ROUTE_TILE), lambda t, bs: (0, t))],
        out_specs=pl.BlockSpec(memory_space=pl.ANY),
        scratch_shapes=[pltpu.VMEM((2, N_EXPERTS * CHUNK, XS_W), BF16),
                        pltpu.VMEM((N_EXPERTS, CHUNK, XS_W), BF16),
                        pltpu.VMEM((N_EXPERTS, ROW_ALIGN, XS_W), BF16),
                        pltpu.SemaphoreType.DMA((2, N_EXPERTS)),
                        pltpu.SemaphoreType.DMA((N_EXPERTS,)),
                        pltpu.SMEM((N_EXPERTS,), I32)])
    return pl.pallas_call(
        _compact_kernel,
        grid_spec=grid_spec,
        out_shape=jax.ShapeDtypeStruct((N_EXPERTS, cap + CHUNK, XS_W), BF16),
        compiler_params=_cparams(("arbitrary",)),
        name="compact",
    )(bs, h2, g3, selpos)


def _ffn_kernel(xs_ref, wg_ref, wu_ref, wd_ref, ys_ref, wgb_ref, wub_ref, wdb_ref):
    e = pl.program_id(0)

    @pl.when(pl.program_id(1) == 0)
    def _():
        wgb_ref[...] = wg_ref[0].astype(BF16)
        wub_ref[...] = wu_ref[0].astype(BF16)
        wdb_ref[...] = wd_ref[0].astype(BF16)

    xs = xs_ref[0]
    x = xs[:, 0:D]
    lane = lax.broadcasted_iota(I32, (1, LANES), 1)
    mine = jnp.logical_or(lane == e, jnp.logical_or(lane == e + N_EXPERTS, lane == e + 2 * N_EXPERTS))
    gate = jnp.sum(jnp.where(mine, xs[:, D:XS_W].astype(F32), 0.0), axis=1, keepdims=True)
    a = _dot(x, wgb_ref[...])
    hid = (a * _sigmoid(a) * _dot(x, wub_ref[...])).astype(BF16)
    ys_ref[0] = (_dot(hid, wdb_ref[...]) * gate).astype(BF16)


def _ffn(xs, w_gate, w_up, w_down, cap):
    tm = min(512, cap)
    assert cap % tm == 0
    wspec = lambda shape: pl.BlockSpec(shape, lambda e, j: (e, 0, 0))
    return pl.pallas_call(
        _ffn_kernel,
        grid=(N_EXPERTS, cap // tm),
        in_specs=[pl.BlockSpec((1, tm, XS_W), lambda e, j: (e, j, 0)),
                  wspec((1, D, FF)), wspec((1, D, FF)), wspec((1, FF, D))],
        out_specs=pl.BlockSpec((1, tm, D), lambda e, j: (e, j, 0)),
        out_shape=jax.ShapeDtypeStruct((N_EXPERTS, cap, D), BF16),
        scratch_shapes=[pltpu.VMEM((D, FF), BF16), pltpu.VMEM((D, FF), BF16), pltpu.VMEM((FF, D), BF16)],
        compiler_params=_cparams(("arbitrary", "arbitrary")),
        name="ffn",
    )(xs, w_gate, w_up, w_down)


def _combine_kernel(bs_ref, x1_ref, post_ref, ada_ref, nw_ref, expand_ref, ys_ref, o_ref,
                    strips_ref, over_ref, acc_ref, sems, osems, *, cap):
    nts = pl.num_programs(1)
    t = pl.program_id(0) * nts + pl.program_id(1)
    nt = pl.num_programs(0) * nts

    def strip_base(e, tile):
        _, _, base, _ = _row_window(bs_ref, e, tile)
        return jnp.minimum(base, cap - CHUNK)

    def strip_copy(e, tile, slot):
        src = ys_ref.at[e, pl.ds(pl.multiple_of(strip_base(e, tile), ROW_ALIGN), CHUNK), :]
        return pltpu.make_async_copy(src, strips_ref.at[slot, pl.ds(e * CHUNK, CHUNK), :], sems.at[slot, e])

    def over_copy(e, cb):
        return pltpu.make_async_copy(ys_ref.at[e, pl.ds(pl.multiple_of(cb, ROW_ALIGN), CHUNK), :], over_ref.at[e],
                                     osems.at[e])

    def n_strips(e):
        _, cnt, _, off = _row_window(bs_ref, e, t)
        return (off + cnt + CHUNK - 1) // CHUNK

    def over_base(e, cidx):
        _, _, base, _ = _row_window(bs_ref, e, t)
        return jnp.minimum(base + cidx * CHUNK, cap - CHUNK)

    @pl.when(t == 0)
    def _():
        for e in range(N_EXPERTS):
            strip_copy(e, 0, 0).start()

    @pl.when(t + 1 < nt)
    def _():
        for e in range(N_EXPERTS):
            strip_copy(e, t + 1, (t + 1) % 2).start()

    for e in range(N_EXPERTS):
        @pl.when(n_strips(e) >= 2)
        def _():
            over_copy(e, over_base(e, 1)).start()

    pos = post_ref[...]
    p_hi = jnp.floor(pos * (1.0 / CHUNK))
    p_lo = pos - p_hi * CHUNK
    hi_x = _dot(p_hi.astype(BF16), expand_ref[...])
    lo_x = _dot(p_lo.astype(BF16), expand_ref[...])
    lane = lax.broadcasted_iota(I32, (1, N_EXPERTS * CHUNK), 1)
    want = lane % CHUNK
    for e in range(N_EXPERTS):
        want = want + jnp.where(lane // CHUNK == e, strip_base(e, t), 0)
    w_hi = (want // CHUNK).astype(F32)
    w_lo = (want % CHUNK).astype(F32)
    onehot = jnp.where(jnp.logical_and(hi_x == w_hi, lo_x == w_lo), 1.0, 0.0).astype(BF16)

    slot = t % 2
    for e in range(N_EXPERTS):
        strip_copy(e, t, slot).wait()
    acc_ref[...] = _dot(onehot, strips_ref[slot])

    lane_c = lax.broadcasted_iota(I32, (1, CHUNK), 1)
    for e in range(N_EXPERTS):
        _, _, base, _ = _row_window(bs_ref, e, t)
        nch = n_strips(e)

        def more(cidx, _):
            cb = over_base(e, cidx)

            @pl.when(cidx >= 2)
            def _():
                over_copy(e, cb).start()

            tgt = (lane_c + cb).astype(F32)
            prev_end = (jnp.minimum(base + (cidx - 1) * CHUNK, cap - CHUNK) + CHUNK).astype(F32)
            col = pos[:, e:e + 1]
            oh = jnp.where(jnp.logical_and(col == tgt, col >= prev_end), 1.0, 0.0).astype(BF16)
            over_copy(e, cb).wait()
            acc_ref[...] += _dot(oh, over_ref[e])
            return 0

        lax.fori_loop(1, nch, more, 0)

    y = acc_ref[...]
    o_ref[0] = x1_ref[0] + _rms(y, ada_ref[0, 5:6, :] * nw_ref[...])


def _combine(bs, x1, post, ada, norm_w, expand, ys, cap):
    B, S, _ = x1.shape
    nts = S // ROUTE_TILE
    grid_spec = pltpu.PrefetchScalarGridSpec(
        num_scalar_prefetch=1,
        grid=(B, nts),
        in_specs=[pl.BlockSpec((1, ROUTE_TILE, D), lambda b, i, bs: (b, i, 0)),
                  pl.BlockSpec((ROUTE_TILE, LANES), lambda b, i, bs: (b * nts + i, 0)),
                  pl.BlockSpec((1, 6, D), lambda b, i, bs: (b, 0, 0)),
                  pl.BlockSpec((1, D), lambda b, i, bs: (0, 0)),
                  pl.BlockSpec((LANES, N_EXPERTS * CHUNK), lambda b, i, bs: (0, 0)),
                  pl.BlockSpec(memory_space=pl.ANY)],
        out_specs=pl.BlockSpec((1, ROUTE_TILE, D), lambda b, i, bs: (b, i, 0)),
        scratch_shapes=[pltpu.VMEM((2, N_EXPERTS * CHUNK, D), BF16),
                        pltpu.VMEM((N_EXPERTS, CHUNK, D), BF16),
                        pltpu.VMEM((ROUTE_TILE, D), F32),
                        pltpu.SemaphoreType.DMA((2, N_EXPERTS)),
                        pltpu.SemaphoreType.DMA((N_EXPERTS,))])
    return pl.pallas_call(
        functools.partial(_combine_kernel, cap=cap),
        grid_spec=grid_spec,
        out_shape=jax.ShapeDtypeStruct((B, S, D), F32),
        compiler_params=_cparams(("arbitrary", "arbitrary")),
        name="combine",
    )(bs, x1, post, ada, norm_w.reshape(1, D), expand, ys)


def _prepare_weights(w_in, w_branch_attn, w_branch_gla, w_out, w_router, w_gate_e, w_up_e, w_down_e):
    aq, ak, av, gq, gk, gv, gr, glf, glb, ga, gg = jnp.split(
        w_in, np.cumsum([512, 128, 128, 256, 256, 512, 512, 16, 16, 1024]).tolist(), axis=1)
    pad = jnp.zeros((D, NP - C_GL - 2 * GLA_RANK), F32)
    w_in_p = jnp.concatenate([aq, gv, ga, gg, gr, ak, av, gq, gk, glf, glb, pad], axis=1).astype(BF16)
    wr = jnp.zeros((D, LANES), F32).at[:, 0:N_EXPERTS].set(w_router)
    wr_hi = wr.astype(BF16)
    wr_cat = jnp.concatenate([wr_hi, (wr - wr_hi.astype(F32)).astype(BF16)], axis=1)
    expand = np.zeros((LANES, N_EXPERTS * CHUNK), np.float32)
    for e in range(N_EXPERTS):
        expand[e, e * CHUNK:(e + 1) * CHUNK] = 1.0
    return dict(w_in_p=w_in_p, w_ba=w_branch_attn.astype(BF16), w_bg=w_branch_gla.astype(BF16),
                w_out=w_out.astype(BF16), wr=wr_cat, w_gate=w_gate_e,
                w_up=w_up_e, w_down=w_down_e, expand=jnp.asarray(expand, BF16),
                bias=_attn_bias_table())


def _layer(x, ada, w, p):
    B, S, _ = x.shape
    n_tok = B * S
    cap = CAPACITY_FACTOR * n_tok // N_EXPERTS
    assert n_tok % ROUTE_TILE == 0 and S % ROUTE_TILE == 0 and cap % CHUNK == 0
    proj = _inproj(x, ada, p['norm_pre_mix'], w['w_in_p'])
    attn = _attention(proj, p['attn_sink'], w['bias'])
    o_g = _gla(proj, p['gla_wa2_fwd'], p['gla_ba_fwd'], p['gla_wa2_bwd'], p['gla_ba_bwd'], p['gla_norm'])
    x1, h2, g3, aff_t = _post(x, attn, o_g, proj, ada, w['w_ba'], w['w_bg'], w['w_out'],
                              p['norm_post_mix'], p['norm_pre_ffn'], w['wr'])
    selpos, post, bs = _select(aff_t, cap)
    xs = _compact(bs, h2.reshape(n_tok, D), g3.reshape(n_tok, LANES), selpos, cap)
    ys = _ffn(xs, w['w_gate'], w['w_up'], w['w_down'], cap)
    return _combine(bs, x1, post, ada, p['norm_post_ffn'], w['expand'], ys, cap)


def kernel(x_prompt, x_sample, c_prompt, c_sample, w_ada, b_ada, norm_pre_mix, norm_post_mix, w_in, attn_sink, gla_wa2_fwd, gla_ba_fwd, gla_wa2_bwd, gla_ba_bwd, gla_norm, w_branch_attn, w_branch_gla, w_out, norm_pre_ffn, norm_post_ffn, w_router, w_gate_e, w_up_e, w_down_e):
    assert w_ada.shape[0] == 1, "one layer"
    p = dict(norm_pre_mix=norm_pre_mix[0], norm_post_mix=norm_post_mix[0], attn_sink=attn_sink[0],
             gla_wa2_fwd=gla_wa2_fwd[0], gla_ba_fwd=gla_ba_fwd[0], gla_wa2_bwd=gla_wa2_bwd[0],
             gla_ba_bwd=gla_ba_bwd[0], gla_norm=gla_norm[0], norm_pre_ffn=norm_pre_ffn[0],
             norm_post_ffn=norm_post_ffn[0])
    w = _prepare_weights(w_in[0], w_branch_attn[0], w_branch_gla[0], w_out[0], w_router[0],
                         w_gate_e[0], w_up_e[0], w_down_e[0])
    bp, bs_ = c_prompt.shape[0], c_sample.shape[0]
    rows = -(-(bp + bs_) // 8) * 8
    c_all = jnp.concatenate([c_prompt, c_sample, jnp.zeros((rows - bp - bs_, D), F32)], axis=0)
    ada = _ada(c_all, w_ada[0], b_ada[0]).reshape(rows, 6, D)
    y_prompt = _layer(x_prompt, ada[0:bp], w, p)
    y_sample = _layer(x_sample, ada[bp:bp + bs_], w, p)
    return (y_prompt, y_sample)
```

```python
import functools

import numpy as np
import jax
import jax.numpy as jnp
from jax import lax
from jax.experimental import pallas as pl
from jax.experimental.pallas import tpu as pltpu

F32, BF16, I32 = jnp.float32, jnp.bfloat16, jnp.int32

D = 1024
N_Q_HEADS, N_KV_HEADS, HEAD_DIM = 8, 2, 64
ATTN_BLOCK = 128
GLA_HEADS, GLA_K, GLA_V = 4, 64, 128
GLA_RANK = 16
GLA_NORMALIZER = 16.0
N_EXPERTS, FF = 16, 1024
CAPACITY_FACTOR = 2
EPS = 1e-6
NEG = -1e30
LOG2E = 1.4426950408889634

C_AQ, C_GV, C_GA, C_GG, C_GR, C_KV, C_GQ, C_GK, C_GL = 0, 512, 1024, 2048, 3072, 3584, 3840, 4096, 4352
NP = 4480

LANES = 128
ROW_ALIGN = 16
CHUNK = 64
ROUTE_TILE = 256
XS_W = D + LANES
VMEM_LIMIT = 56 * 1024 * 1024


def _cparams(sem):
    return pltpu.CompilerParams(dimension_semantics=sem, vmem_limit_bytes=VMEM_LIMIT)


def _dot(a, b):
    return jnp.dot(a, b, preferred_element_type=F32)


def _dot_nt(a, b):
    return lax.dot_general(a, b, (((1,), (1,)), ((), ())), preferred_element_type=F32)


def _dot_tn(a, b):
    return lax.dot_general(a, b, (((0,), (0,)), ((), ())), preferred_element_type=F32)


def _split2(x):
    hi = x.astype(BF16)
    lo = (x - hi.astype(F32)).astype(BF16)
    return hi, lo


def _split3(x):
    hi = x.astype(BF16)
    r = x - hi.astype(F32)
    mid = r.astype(BF16)
    lo = (r - mid.astype(F32)).astype(BF16)
    return hi, mid, lo


def _sigmoid(x):
    return 0.5 * jnp.tanh(0.5 * x) + 0.5


def _rms(x, w):
    return x * lax.rsqrt(jnp.mean(x * x, axis=-1, keepdims=True) + EPS) * w


def _ada_kernel(c_ref, w_ref, b_ref, o_ref):
    c = c_ref[...]
    s = c * jax.nn.sigmoid(c)
    s_hi, s_lo = _split2(s)
    w_hi, w_lo = _split2(w_ref[...])
    o_ref[...] = _dot(s_hi, w_hi) + _dot(s_lo, w_hi) + _dot(s_hi, w_lo) + b_ref[...]


def _ada(c, w_ada, b_ada):
    rows = c.shape[0]
    tn = 1024
    return pl.pallas_call(
        _ada_kernel,
        grid=(6 * D // tn,),
        in_specs=[pl.BlockSpec((rows, D), lambda j: (0, 0)),
                  pl.BlockSpec((D, tn), lambda j: (0, j)),
                  pl.BlockSpec((1, tn), lambda j: (0, j))],
        out_specs=pl.BlockSpec((rows, tn), lambda j: (0, j)),
        out_shape=jax.ShapeDtypeStruct((rows, 6 * D), F32),
        compiler_params=_cparams(("arbitrary",)),
        name="ada",
    )(c, w_ada, b_ada.reshape(1, 6 * D))


def _inproj_kernel(x_ref, ada_ref, nw_ref, w_ref, o_ref):
    x = x_ref[0]
    h = (_rms(x, nw_ref[...] * (1.0 + ada_ref[0, 1:2, :])) + ada_ref[0, 0:1, :]).astype(BF16)
    step = 1024
    for c0 in range(0, NP, step):
        c1 = min(c0 + step, NP)
        o_ref[0, :, c0:c1] = _dot(h, w_ref[:, c0:c1]).astype(BF16)


def _inproj(x, ada, norm_w, w_in_p):
    B, S, _ = x.shape
    tm = 1024
    return pl.pallas_call(
        _inproj_kernel,
        grid=(B, S // tm),
        in_specs=[pl.BlockSpec((1, tm, D), lambda b, i: (b, i, 0)),
                  pl.BlockSpec((1, 6, D), lambda b, i: (b, 0, 0)),
                  pl.BlockSpec((1, D), lambda b, i: (0, 0)),
                  pl.BlockSpec((D, NP), lambda b, i: (0, 0), pipeline_mode=pl.Buffered(1))],
        out_specs=pl.BlockSpec((1, tm, NP), lambda b, i: (b, i, 0)),
        out_shape=jax.ShapeDtypeStruct((B, S, NP), BF16),
        compiler_params=_cparams(("arbitrary", "arbitrary")),
        name="inproj",
    )(x, ada, norm_w.reshape(1, D), w_in_p)


ATTN_TQ = 2048
ATTN_QB = ATTN_TQ // ATTN_BLOCK


_ATTN_STACK = (0, 2, 1, 3)


def _attn_bias_table():
    L = ATTN_BLOCK
    group = N_Q_HEADS // N_KV_HEADS
    q = np.arange(L)[:, None]
    k = np.arange(3 * L)[None, :]
    dist = np.abs(q - k + L).astype(np.float32)
    slopes = np.asarray([2.0 ** (-8.0 * (h + 1) / N_Q_HEADS) for h in range(N_Q_HEADS)], np.float32)
    base = -slopes[:, None, None] * dist[None] * np.float32(LOG2E)
    inside = (dist <= L)[None]
    variants = []
    for ok_k in (np.ones_like(k, bool), k >= L, k < 2 * L):
        per_head = np.where(inside & ok_k[None], base, np.float32(NEG))
        variants.append(np.stack([np.concatenate([per_head[hkv * group + g] for g in _ATTN_STACK], axis=0)
                                  for hkv in range(N_KV_HEADS)]))
    return jnp.asarray(np.stack(variants), F32)


def _attn_kernel(sink_ref, q_ref, kvm_ref, kvp_ref, kvn_ref, bias_ref, o_ref):
    L = ATTN_BLOCK
    i = pl.program_id(1)
    last = pl.num_programs(1) - 1
    kv = jnp.concatenate([kvp_ref[0], kvm_ref[0], kvn_ref[0]], axis=0)
    k = kv[:, :LANES] * jnp.asarray(HEAD_DIM ** -0.5, BF16)
    v = kv[:, LANES:]
    lo = lax.broadcasted_iota(I32, (1, LANES), 1) < HEAD_DIM
    zero = jnp.zeros_like(k)

    def halves(t):
        sw = jnp.concatenate([t[:, HEAD_DIM:], t[:, :HEAD_DIM]], axis=1)
        return ((jnp.where(lo, t, zero), jnp.where(lo, zero, sw)),
                (jnp.where(lo, sw, zero), jnp.where(lo, zero, t)))

    kh, vh = halves(k), halves(v)
    group = N_Q_HEADS // N_KV_HEADS
    rowblk = lax.broadcasted_iota(I32, (group * L, 1), 0) // L
    for j in range(ATTN_QB):
        first = jnp.logical_and(i == 0, j == 0)
        final = jnp.logical_and(i == last, j == ATTN_QB - 1)
        var = jnp.where(first, 1, jnp.where(final, 2, 0))
        r0 = j * L
        for hkv in range(N_KV_HEADS):
            c0 = hkv * 2 * LANES
            qs = jnp.concatenate([q_ref[0, r0:r0 + L, c0:c0 + LANES],
                                  q_ref[0, r0:r0 + L, c0 + LANES:c0 + 2 * LANES]], axis=0)
            s = jnp.concatenate([_dot_nt(qs, kh[hkv][0][r0:r0 + 3 * L]),
                                 _dot_nt(qs, kh[hkv][1][r0:r0 + 3 * L])], axis=0) + bias_ref[var, hkv]
            snk = jnp.zeros((group * L, 1), F32)
            for b, g in enumerate(_ATTN_STACK):
                snk = jnp.where(rowblk == b, sink_ref[hkv * group + g], snk)
            mx = jnp.maximum(jnp.max(s, axis=-1, keepdims=True), snk)
            p = jnp.exp2(s - mx)
            rden = 1.0 / (jnp.sum(p, axis=-1, keepdims=True) + jnp.exp2(snk - mx))
            pb = p.astype(BF16)
            o = (_dot(pb[0:2 * L], vh[hkv][0][r0:r0 + 3 * L]) * rden[0:2 * L]
                 + _dot(pb[2 * L:4 * L], vh[hkv][1][r0:r0 + 3 * L]) * rden[2 * L:4 * L])
            o_ref[0, r0:r0 + L, c0:c0 + LANES] = o[0:L].astype(BF16)
            o_ref[0, r0:r0 + L, c0 + LANES:c0 + 2 * LANES] = o[L:2 * L].astype(BF16)


def _attention(proj, sink, bias):
    B, S, _ = proj.shape
    L = ATTN_BLOCK
    nb = S // L
    assert S % ATTN_TQ == 0 and nb >= 2
    kvc = C_KV // 256
    return pl.pallas_call(
        _attn_kernel,
        grid=(B, S // ATTN_TQ),
        in_specs=[pl.BlockSpec(memory_space=pltpu.SMEM),
                  pl.BlockSpec((1, ATTN_TQ, 512), lambda b, i: (b, i, C_AQ // 512)),
                  pl.BlockSpec((1, ATTN_TQ, 256), lambda b, i: (b, i, kvc)),
                  pl.BlockSpec((1, L, 256), lambda b, i: (b, jnp.maximum(i * ATTN_QB - 1, 0), kvc)),
                  pl.BlockSpec((1, L, 256), lambda b, i: (b, jnp.minimum(i * ATTN_QB + ATTN_QB, nb - 1), kvc)),
                  pl.BlockSpec((3, N_KV_HEADS, (N_Q_HEADS // N_KV_HEADS) * L, 3 * L), lambda b, i: (0, 0, 0, 0))],
        out_specs=pl.BlockSpec((1, ATTN_TQ, 512), lambda b, i: (b, i, 0)),
        out_shape=jax.ShapeDtypeStruct((B, S, 512), BF16),
        compiler_params=_cparams(("arbitrary", "arbitrary")),
        name="attn",
    )(sink * LOG2E, proj, proj, proj, proj, bias)


GLA_TS = 2048
GLA_C = 128
GLA_NC = GLA_TS // GLA_C


def _gla_tile(q_ref, k_ref, v_ref, gl_ref, w_ref, ba_ref, st_ref, rev):
    C = GLA_C
    x = _dot(gl_ref[0], w_ref[...]) + ba_ref[...]
    la = (jnp.minimum(x, 0.0) - jnp.log(1.0 + jnp.exp(-jnp.abs(x)))) * (1.0 / GLA_NORMALIZER)
    row = lax.broadcasted_iota(I32, (C, C), 0)
    col = lax.broadcasted_iota(I32, (C, C), 1)
    if rev:
        tri = jnp.where(col >= row, 1.0, 0.0).astype(BF16)
        keep = col > row
    else:
        tri = jnp.where(col <= row, 1.0, 0.0).astype(BF16)
        keep = col <= row
    hi, mid, lo3 = _split3(la)
    bcs, halves, ehs = [], [], []
    for c in range(GLA_NC):
        rs = slice(c * C, (c + 1) * C)
        bc = _dot(tri, hi[rs]) + _dot(tri, mid[rs]) + _dot(tri, lo3[rs])
        half = 0.5 * (bc[0:1] if rev else bc[C - 1:C])
        bcs.append(bc)
        halves.append(jnp.broadcast_to(half, bc.shape))
        ehs.append(jnp.exp(half))
    bc = jnp.concatenate(bcs, axis=0)
    half = jnp.concatenate(halves, axis=0)
    eh = jnp.concatenate([jnp.broadcast_to(e, (C, e.shape[1])) for e in ehs], axis=0)
    qe = q_ref[0].astype(F32) * (GLA_K ** -0.5) * jnp.exp(bc - half)
    ke = k_ref[0].astype(F32) * jnp.exp(half - bc)
    qt, kt = qe.astype(BF16), ke.astype(BF16)
    qh = (qe * eh).astype(BF16)
    kb = (ke * eh).astype(BF16)
    lo = lax.broadcasted_iota(I32, (1, LANES), 1) < GLA_K
    zero = jnp.zeros((GLA_TS, LANES), BF16)
    order = range(GLA_NC - 1, -1, -1) if rev else range(GLA_NC)
    outs = [[None] * GLA_HEADS for _ in range(GLA_NC)]
    for m in range(GLA_HEADS // 2):
        sl = slice(m * LANES, (m + 1) * LANES)
        ktm = (jnp.where(lo, kt[:, sl], zero), jnp.where(lo, zero, kt[:, sl]))
        qhm = (jnp.where(lo, qh[:, sl], zero), jnp.where(lo, zero, qh[:, sl]))
        st = st_ref[m]
        for c in order:
            rs = slice(c * C, (c + 1) * C)
            stb = st.astype(BF16)
            upd = []
            for par in range(2):
                h = 2 * m + par
                a = jnp.where(keep, _dot_nt(qt[rs, sl], ktm[par][rs]), 0.0).astype(BF16)
                vh = v_ref[0, rs, h * GLA_V:(h + 1) * GLA_V]
                outs[c][h] = _dot(a, vh) + _dot_nt(qhm[par][rs], stb)
                upd.append(_dot_tn(vh, kb[rs, sl]))
            dec = ehs[c][:, sl] * ehs[c][:, sl]
            st = st * dec + jnp.where(lo, upd[0], upd[1])
        st_ref[m] = st
    return outs


def _gla_fwd_kernel(q_ref, k_ref, v_ref, gl_ref, w_ref, ba_ref, o_ref, st_ref):
    @pl.when(pl.program_id(1) == 0)
    def _():
        st_ref[...] = jnp.zeros_like(st_ref)

    outs = _gla_tile(q_ref, k_ref, v_ref, gl_ref, w_ref, ba_ref, st_ref, rev=False)
    for c in range(GLA_NC):
        for h in range(GLA_HEADS):
            o_ref[0, c * GLA_C:(c + 1) * GLA_C, h * GLA_V:(h + 1) * GLA_V] = outs[c][h].astype(BF16)


def _gla_bwd_kernel(q_ref, k_ref, v_ref, gl_ref, w_ref, ba_ref, of_ref, gr_ref, nw_ref, o_ref, st_ref):
    @pl.when(pl.program_id(1) == 0)
    def _():
        st_ref[...] = jnp.zeros_like(st_ref)

    outs = _gla_tile(q_ref, k_ref, v_ref, gl_ref, w_ref, ba_ref, st_ref, rev=True)
    for h in range(GLA_HEADS):
        hs = slice(h * GLA_V, (h + 1) * GLA_V)
        o = jnp.concatenate([outs[c][h] for c in range(GLA_NC)], axis=0) + of_ref[0, :, hs].astype(F32)
        g = gr_ref[0, :, hs].astype(F32)
        o_ref[0, :, hs] = (_rms(o, nw_ref[...]) * (g * _sigmoid(g))).astype(BF16)


def _gla(proj, wa2_f, ba_f, wa2_b, ba_b, gla_norm):
    B, S, _ = proj.shape
    nt = S // GLA_TS
    assert S % GLA_TS == 0
    kw = GLA_HEADS * GLA_K
    wf = jnp.zeros((LANES, kw), F32).at[0:GLA_RANK].set(wa2_f).astype(BF16)
    wb = jnp.zeros((LANES, kw), F32).at[GLA_RANK:2 * GLA_RANK].set(wa2_b).astype(BF16)

    def specs(tile):
        return [pl.BlockSpec((1, GLA_TS, 256), lambda b, i: (b, tile(i), C_GQ // 256)),
                pl.BlockSpec((1, GLA_TS, 256), lambda b, i: (b, tile(i), C_GK // 256)),
                pl.BlockSpec((1, GLA_TS, 512), lambda b, i: (b, tile(i), C_GV // 512)),
                pl.BlockSpec((1, GLA_TS, LANES), lambda b, i: (b, tile(i), C_GL // LANES)),
                pl.BlockSpec((LANES, kw), lambda b, i: (0, 0)),
                pl.BlockSpec((1, kw), lambda b, i: (0, 0))]

    scratch = [pltpu.VMEM((GLA_HEADS // 2, GLA_V, LANES), F32)]
    fwd_tile = lambda i: i
    o_f = pl.pallas_call(
        _gla_fwd_kernel,
        grid=(B, nt),
        in_specs=specs(fwd_tile),
        out_specs=pl.BlockSpec((1, GLA_TS, 512), lambda b, i: (b, i, 0)),
        out_shape=jax.ShapeDtypeStruct((B, S, 512), BF16),
        scratch_shapes=scratch,
        compiler_params=_cparams(("arbitrary", "arbitrary")),
        name="gla_fwd",
    )(proj, proj, proj, proj, wf, ba_f.reshape(1, kw))
    bwd_tile = lambda i: nt - 1 - i
    return pl.pallas_call(
        _gla_bwd_kernel,
        grid=(B, nt),
        in_specs=specs(bwd_tile) + [
            pl.BlockSpec((1, GLA_TS, 512), lambda b, i: (b, bwd_tile(i), 0)),
            pl.BlockSpec((1, GLA_TS, 512), lambda b, i: (b, bwd_tile(i), C_GR // 512)),
            pl.BlockSpec((1, GLA_V), lambda b, i: (0, 0))],
        out_specs=pl.BlockSpec((1, GLA_TS, 512), lambda b, i: (b, bwd_tile(i), 0)),
        out_shape=jax.ShapeDtypeStruct((B, S, 512), BF16),
        scratch_shapes=scratch,
        compiler_params=_cparams(("arbitrary", "arbitrary")),
        name="gla_bwd",
    )(proj, proj, proj, proj, wb, ba_b.reshape(1, kw), o_f, proj, gla_norm.reshape(1, GLA_V))


POST_SUB = 256

def _post_kernel(x_ref, attn_ref, og_ref, ga_ref, gg_ref, ada_ref, wba_ref, wbg_ref, wo_ref, npost_ref,
                 npre_ref, wr_ref, x1_ref, h2_ref, g3_ref, afft_ref):
    w_post = ada_ref[0, 2:3, :] * npost_ref[...]
    w_pre = (1.0 + ada_ref[0, 4:5, :]) * npre_ref[...]
    shift = ada_ref[0, 3:4, :]
    for r0 in range(0, x_ref.shape[1], POST_SUB):
        rs = slice(r0, r0 + POST_SUB)
        ga = ga_ref[0, rs, :].astype(F32)
        gg = gg_ref[0, rs, :].astype(F32)
        merged = ((jnp.tanh(ga) + 1.0) * _dot(attn_ref[0, rs, :], wba_ref[...])
                  + (jnp.tanh(gg) + 1.0) * _dot(og_ref[0, rs, :], wbg_ref[...]))
        mix = _dot(merged.astype(BF16), wo_ref[...])
        x1 = x_ref[0, rs, :] + _rms(mix, w_post)
        x1_ref[0, rs, :] = x1
        h2 = _rms(x1, w_pre) + shift
        h2_ref[0, rs, :] = h2.astype(BF16)
        h_hi, h_lo = _split2(h2)
        r = _dot(h_hi, wr_ref[...]) + _dot(h_lo, wr_ref[...])
        logits = r[:, 0:LANES] + r[:, LANES:2 * LANES]
        lane = lax.broadcasted_iota(I32, logits.shape, 1)
        logits = jnp.where(lane < N_EXPERTS, logits, NEG)
        e = jnp.exp(logits - jnp.max(logits, axis=-1, keepdims=True))
        aff = e / jnp.sum(e, axis=-1, keepdims=True)
        a_hi, a_mid, a_lo = _split3(aff)
        g3 = (a_hi.astype(F32) + pltpu.roll(a_mid.astype(F32), N_EXPERTS, 1)
              + pltpu.roll(a_lo.astype(F32), 2 * N_EXPERTS, 1))
        g3_ref[0, rs, :] = g3.astype(BF16)
        afft_ref[:, rs] = aff.T[0:N_EXPERTS, :]


def _post(x, attn, o_g, proj, ada, w_ba, w_bg, w_out, n_post, n_pre, wr):
    B, S, _ = x.shape
    tm = 1024
    nt = S // tm
    const = lambda b, i: (0, 0)
    return pl.pallas_call(
        _post_kernel,
        grid=(B, nt),
        in_specs=[pl.BlockSpec((1, tm, D), lambda b, i: (b, i, 0)),
                  pl.BlockSpec((1, tm, 512), lambda b, i: (b, i, 0)),
                  pl.BlockSpec((1, tm, 512), lambda b, i: (b, i, 0)),
                  pl.BlockSpec((1, tm, D), lambda b, i: (b, i, C_GA // D)),
                  pl.BlockSpec((1, tm, D), lambda b, i: (b, i, C_GG // D)),
                  pl.BlockSpec((1, 6, D), lambda b, i: (b, 0, 0)),
                  pl.BlockSpec((512, D), const),
                  pl.BlockSpec((512, D), const),
                  pl.BlockSpec((D, D), const),
                  pl.BlockSpec((1, D), const),
                  pl.BlockSpec((1, D), const),
                  pl.BlockSpec((D, 2 * LANES), const)],
        out_specs=[pl.BlockSpec((1, tm, D), lambda b, i: (b, i, 0)),
                   pl.BlockSpec((1, tm, D), lambda b, i: (b, i, 0)),
                   pl.BlockSpec((1, tm, LANES), lambda b, i: (b, i, 0)),
                   pl.BlockSpec((N_EXPERTS, tm), lambda b, i: (0, b * nt + i))],
        out_shape=[jax.ShapeDtypeStruct((B, S, D), F32),
                   jax.ShapeDtypeStruct((B, S, D), BF16),
                   jax.ShapeDtypeStruct((B, S, LANES), BF16),
                   jax.ShapeDtypeStruct((N_EXPERTS, B * S), F32)],
        compiler_params=_cparams(("arbitrary", "arbitrary")),
        name="post",
    )(x, attn, o_g, proj, proj, ada, w_ba, w_bg, w_out, n_post.reshape(1, D), n_pre.reshape(1, D), wr)


def _select_kernel(aff_ref, selpos_ref, post_ref, bs_ref, *, cap, n_tok):
    nblk = n_tok // LANES

    def count(mask):
        return jnp.sum(jnp.where(mask, 1.0, 0.0), axis=1, keepdims=True).astype(I32)

    def bit_body(it, bits):
        cand = bits | jnp.left_shift(jnp.int32(1), 30 - it)
        return jnp.where(count(aff_ref[...] >= pltpu.bitcast(cand, F32)) >= cap, cand, bits)

    thr = pltpu.bitcast(lax.fori_loop(0, 31, bit_body, jnp.zeros((N_EXPERTS, 1), I32)), F32)
    need = cap - count(aff_ref[...] > thr)
    r = lax.broadcasted_iota(I32, (LANES, LANES), 0)
    c = lax.broadcasted_iota(I32, (LANES, LANES), 1)
    scan_mat = jnp.concatenate([jnp.where(r <= c, 1.0, 0.0), jnp.ones((LANES, LANES), F32)], axis=1).astype(BF16)
    blk_lane = lax.broadcasted_iota(I32, bs_ref.shape, 1)
    filler = jnp.full((LANES - N_EXPERTS, LANES), -1.0, F32)
    reps = bs_ref.shape[1] // LANES

    def taken_before(c_gt, c_eq):
        return jnp.concatenate([c_gt + jnp.minimum(c_eq, need)] * reps, axis=1)

    def blk_body(j, carry):
        c_gt, c_eq, bs = carry
        off = pl.multiple_of(j * LANES, LANES)
        a = aff_ref[:, pl.ds(off, LANES)]
        gt = a > thr
        eq = a == thr
        both = jnp.concatenate([jnp.where(gt, 1.0, 0.0), jnp.where(eq, 1.0, 0.0)], axis=0).astype(BF16)
        scan = _dot(both, scan_mat).astype(I32)
        ex_gt = c_gt + scan[0:N_EXPERTS, 0:LANES] - jnp.where(gt, 1, 0)
        ex_eq = c_eq + scan[N_EXPERTS:, 0:LANES] - jnp.where(eq, 1, 0)
        sel = jnp.logical_or(gt, jnp.logical_and(eq, ex_eq < need))
        pos = ex_gt + jnp.minimum(ex_eq, need)
        sp = jnp.where(sel, pos, -1)
        selpos_ref[:, pl.ds(off, LANES)] = sp
        post_ref[pl.ds(off, LANES), :] = jnp.concatenate([sp.astype(F32), filler], axis=0).T
        bs = jnp.where(blk_lane == j, taken_before(c_gt, c_eq), bs)
        return c_gt + scan[0:N_EXPERTS, LANES:], c_eq + scan[N_EXPERTS:, LANES:], bs

    z = jnp.zeros((N_EXPERTS, LANES), I32)
    c_gt, c_eq, bs = lax.fori_loop(0, nblk, blk_body, (z, z, jnp.zeros(bs_ref.shape, I32)), unroll=2)
    bs_ref[...] = jnp.where(blk_lane >= nblk, taken_before(c_gt, c_eq), bs)


def _select(aff_t, cap):
    n_tok = aff_t.shape[1]
    nbp = ((n_tok // LANES + 1 + LANES - 1) // LANES) * LANES
    return pl.pallas_call(
        functools.partial(_select_kernel, cap=cap, n_tok=n_tok),
        out_shape=[jax.ShapeDtypeStruct((N_EXPERTS, n_tok), I32),
                   jax.ShapeDtypeStruct((n_tok, LANES), F32),
                   jax.ShapeDtypeStruct((N_EXPERTS, nbp), I32)],
        compiler_params=pltpu.CompilerParams(vmem_limit_bytes=VMEM_LIMIT),
        name="select",
    )(aff_t)


def _row_window(bs_ref, e, tile):
    per = ROUTE_TILE // LANES
    start = bs_ref[e, tile * per]
    cnt = bs_ref[e, tile * per + per] - start
    base = (start // ROW_ALIGN) * ROW_ALIGN
    return start, cnt, base, start - base


def _compact_kernel(bs_ref, h2_ref, g3_ref, selpos_ref, xs_ref, stage_ref, over_ref, carry_ref, sems, osems, pend_ref):
    t = pl.program_id(0)
    nt = pl.num_programs(0)
    par = t % 2
    cap = xs_ref.shape[1] - CHUNK

    def strip_copy(slot, e, base):
        return pltpu.make_async_copy(stage_ref.at[slot, pl.ds(e * CHUNK, CHUNK), :],
                                     xs_ref.at[e, pl.ds(pl.multiple_of(base, ROW_ALIGN), CHUNK), :], sems.at[slot, e])

    def over_copy(e, base):
        return pltpu.make_async_copy(over_ref.at[e], xs_ref.at[e, pl.ds(pl.multiple_of(base, ROW_ALIGN), CHUNK), :],
                                     osems.at[e])

    @pl.when(t == 0)
    def _():
        over_ref[0] = jnp.zeros((CHUNK, XS_W), BF16)
        carry_ref[...] = jnp.zeros_like(carry_ref)
        for e in range(N_EXPERTS):
            pend_ref[e] = 0
            pltpu.make_async_copy(over_ref.at[0], xs_ref.at[e, pl.ds(cap, CHUNK), :], osems.at[e]).start()
        for e in range(N_EXPERTS):
            pltpu.make_async_copy(over_ref.at[0], xs_ref.at[e, pl.ds(cap, CHUNK), :], osems.at[e]).wait()

    jrow = lax.broadcasted_iota(I32, (CHUNK, ROUTE_TILE), 0)
    pieces = []
    for e in range(N_EXPERTS):
        _, _, base, _ = _row_window(bs_ref, e, t)
        pieces.append(jnp.where(selpos_ref[e:e + 1, :] - base == jrow, 1.0, 0.0).astype(BF16))
    onehot = jnp.concatenate(pieces, axis=0)
    stage_ref[par, :, 0:D] = _dot(onehot, h2_ref[...]).astype(BF16)
    stage_ref[par, :, D:XS_W] = _dot(onehot, g3_ref[...]).astype(BF16)

    r16 = lax.broadcasted_iota(I32, (ROW_ALIGN, XS_W), 0)
    for e in range(N_EXPERTS):
        start, cnt, base, off = _row_window(bs_ref, e, t)
        head = pl.ds(e * CHUNK, ROW_ALIGN)
        stage_ref[par, head, :] = jnp.where(r16 < off, carry_ref[e], stage_ref[par, head, :])

        @pl.when(t > 0)
        def _():
            strip_copy(1 - par, e, 0).wait()

        @pl.when(pend_ref[e] == 1)
        def _():
            over_copy(e, 0).wait()
            pend_ref[e] = 0

        strip_copy(par, e, base).start()
        end = start + cnt
        nch = (off + cnt + CHUNK - 1) // CHUNK
        gbase = (end // ROW_ALIGN) * ROW_ALIGN
        src = pl.multiple_of(e * CHUNK + jnp.minimum(gbase - base, CHUNK - ROW_ALIGN), ROW_ALIGN)
        carry_ref[e] = stage_ref[par, pl.ds(src, ROW_ALIGN), :]

        def more(cidx, _):
            cb = base + cidx * CHUNK

            @pl.when(pend_ref[e] == 1)
            def _():
                over_copy(e, 0).wait()

            oh = jnp.where(selpos_ref[e:e + 1, :] - cb == jrow, 1.0, 0.0).astype(BF16)
            over_ref[e, :, 0:D] = _dot(oh, h2_ref[...]).astype(BF16)
            over_ref[e, :, D:XS_W] = _dot(oh, g3_ref[...]).astype(BF16)
            over_copy(e, cb).start()
            pend_ref[e] = 1

            @pl.when(jnp.logical_and(cidx == nch - 1, gbase - cb < CHUNK))
            def _():
                carry_ref[e] = over_ref[e, pl.ds(pl.multiple_of(gbase - cb, ROW_ALIGN), ROW_ALIGN), :]
            return 0

        lax.fori_loop(1, nch, more, 0)

    @pl.when(t == nt - 1)
    def _():
        for e in range(N_EXPERTS):
            strip_copy(par, e, 0).wait()

            @pl.when(pend_ref[e] == 1)
            def _():
                over_copy(e, 0).wait()
                pend_ref[e] = 0


def _compact(bs, h2, g3, selpos, cap):
    n_tok = h2.shape[0]
    nt = n_tok // ROUTE_TILE
    grid_spec = pltpu.PrefetchScalarGridSpec(
        num_scalar_prefetch=1,
        grid=(nt,),
        in_specs=[pl.BlockSpec((ROUTE_TILE, D), lambda t, bs: (t, 0)),
                  pl.BlockSpec((ROUTE_TILE, LANES), lambda t, bs: (t, 0)),
                  pl.BlockSpec((N_EXPERTS, ROUTE_TILE), lambda t, bs: (0, t))],
        out_specs=pl.BlockSpec(memory_space=pl.ANY),
        scratch_shapes=[pltpu.VMEM((2, N_EXPERTS * CHUNK, XS_W), BF16),
                        pltpu.VMEM((N_EXPERTS, CHUNK, XS_W), BF16),
                        pltpu.VMEM((N_EXPERTS, ROW_ALIGN, XS_W), BF16),
                        pltpu.SemaphoreType.DMA((2, N_EXPERTS)),
                        pltpu.SemaphoreType.DMA((N_EXPERTS,)),
                        pltpu.SMEM((N_EXPERTS,), I32)])
    return pl.pallas_call(
        _compact_kernel,
        grid_spec=grid_spec,
        out_shape=jax.ShapeDtypeStruct((N_EXPERTS, cap + CHUNK, XS_W), BF16),
        compiler_params=_cparams(("arbitrary",)),
        name="compact",
    )(bs, h2, g3, selpos)


def _ffn_kernel(xs_ref, wg_ref, wu_ref, wd_ref, ys_ref, wgb_ref, wub_ref, wdb_ref):
    g = pl.program_id(0)
    j = pl.program_id(1)
    slab = wg_ref.shape[1]

    @pl.when(g < N_EXPERTS)
    def _():
        rows = pl.ds(pl.multiple_of(j * slab, slab), slab)
        wgb_ref[g % 2, rows, :] = wg_ref[0].astype(BF16)
        wub_ref[g % 2, rows, :] = wu_ref[0].astype(BF16)
        wdb_ref[g % 2, rows, :] = wd_ref[0].astype(BF16)

    @pl.when(g > 0)
    def _():
        e = g - 1
        cur = e % 2
        xs = xs_ref[0]
        x = xs[:, 0:D]
        lane = lax.broadcasted_iota(I32, (1, LANES), 1)
        mine = jnp.logical_or(lane == e, jnp.logical_or(lane == e + N_EXPERTS, lane == e + 2 * N_EXPERTS))
        gate = jnp.sum(jnp.where(mine, xs[:, D:XS_W].astype(F32), 0.0), axis=1, keepdims=True)
        a = _dot(x, wgb_ref[cur])
        hid = (a * _sigmoid(a) * _dot(x, wub_ref[cur])).astype(BF16)
        ys_ref[0] = (_dot(hid, wdb_ref[cur]) * gate).astype(BF16)


def _ffn(xs, w_gate, w_up, w_down, cap):
    tm = min(1024, cap)
    nj = cap // tm
    assert cap % tm == 0 and D % nj == 0 and FF % nj == 0
    last = N_EXPERTS - 1
    wmap = lambda g, j: (jnp.minimum(g, last), jnp.where(g <= last, j, nj - 1), 0)
    xmap = lambda g, j: (jnp.maximum(g - 1, 0), jnp.where(g == 0, 0, j), 0)
    return pl.pallas_call(
        _ffn_kernel,
        grid=(N_EXPERTS + 1, nj),
        in_specs=[pl.BlockSpec((1, tm, XS_W), xmap),
                  pl.BlockSpec((1, D // nj, FF), wmap),
                  pl.BlockSpec((1, D // nj, FF), wmap),
                  pl.BlockSpec((1, FF // nj, D), wmap)],
        out_specs=pl.BlockSpec((1, tm, D), xmap),
        out_shape=jax.ShapeDtypeStruct((N_EXPERTS, cap, D), BF16),
        scratch_shapes=[pltpu.VMEM((2, D, FF), BF16), pltpu.VMEM((2, D, FF), BF16), pltpu.VMEM((2, FF, D), BF16)],
        compiler_params=_cparams(("arbitrary", "arbitrary")),
        name="ffn",
    )(xs, w_gate, w_up, w_down)


def _combine_kernel(bs_ref, x1_ref, post_ref, ada_ref, nw_ref, expand_ref, ys_ref, o_ref,
                    strips_ref, over_ref, acc_ref, sems, osems, *, cap):
    nts = pl.num_programs(1)
    t = pl.program_id(0) * nts + pl.program_id(1)
    nt = pl.num_programs(0) * nts

    def strip_base(e, tile):
        _, _, base, _ = _row_window(bs_ref, e, tile)
        return jnp.minimum(base, cap - CHUNK)

    def strip_copy(e, tile, slot):
        src = ys_ref.at[e, pl.ds(pl.multiple_of(strip_base(e, tile), ROW_ALIGN), CHUNK), :]
        return pltpu.make_async_copy(src, strips_ref.at[slot, pl.ds(e * CHUNK, CHUNK), :], sems.at[slot, e])

    def over_copy(e, cb):
        return pltpu.make_async_copy(ys_ref.at[e, pl.ds(pl.multiple_of(cb, ROW_ALIGN), CHUNK), :], over_ref.at[e],
                                     osems.at[e])

    def n_strips(e):
        _, cnt, _, off = _row_window(bs_ref, e, t)
        return (off + cnt + CHUNK - 1) // CHUNK

    def over_base(e, cidx):
        _, _, base, _ = _row_window(bs_ref, e, t)
        return jnp.minimum(base + cidx * CHUNK, cap - CHUNK)

    @pl.when(t == 0)
    def _():
        for e in range(N_EXPERTS):
            strip_copy(e, 0, 0).start()

    @pl.when(t + 1 < nt)
    def _():
        for e in range(N_EXPERTS):
            strip_copy(e, t + 1, (t + 1) % 2).start()

    for e in range(N_EXPERTS):
        @pl.when(n_strips(e) >= 2)
        def _():
            over_copy(e, over_base(e, 1)).start()

    pos = post_ref[...]
    p_hi = jnp.floor(pos * (1.0 / CHUNK))
    p_lo = pos - p_hi * CHUNK
    hi_x = _dot(p_hi.astype(BF16), expand_ref[...])
    lo_x = _dot(p_lo.astype(BF16), expand_ref[...])
    lane = lax.broadcasted_iota(I32, (1, N_EXPERTS * CHUNK), 1)
    want = lane % CHUNK
    for e in range(N_EXPERTS):
        want = want + jnp.where(lane // CHUNK == e, strip_base(e, t), 0)
    w_hi = (want // CHUNK).astype(F32)
    w_lo = (want % CHUNK).astype(F32)
    onehot = jnp.where(jnp.logical_and(hi_x == w_hi, lo_x == w_lo), 1.0, 0.0).astype(BF16)

    slot = t % 2
    for e in range(N_EXPERTS):
        strip_copy(e, t, slot).wait()
    acc_ref[...] = _dot(onehot, strips_ref[slot])

    lane_c = lax.broadcasted_iota(I32, (1, CHUNK), 1)
    for e in range(N_EXPERTS):
        _, _, base, _ = _row_window(bs_ref, e, t)
        nch = n_strips(e)

        def more(cidx, _):
            cb = over_base(e, cidx)

            @pl.when(cidx >= 2)
            def _():
                over_copy(e, cb).start()

            tgt = (lane_c + cb).astype(F32)
            prev_end = (jnp.minimum(base + (cidx - 1) * CHUNK, cap - CHUNK) + CHUNK).astype(F32)
            col = post_ref[:, e:e + 1]
            oh = jnp.where(jnp.logical_and(col == tgt, col >= prev_end), 1.0, 0.0).astype(BF16)
            over_copy(e, cb).wait()
            acc_ref[...] += _dot(oh, over_ref[e])
            return 0

        lax.fori_loop(1, nch, more, 0)

    y = acc_ref[...]
    o_ref[0] = x1_ref[0] + _rms(y, ada_ref[0, 5:6, :] * nw_ref[...])


def _combine(bs, x1, post, ada, norm_w, expand, ys, cap):
    B, S, _ = x1.shape
    nts = S // ROUTE_TILE
    grid_spec = pltpu.PrefetchScalarGridSpec(
        num_scalar_prefetch=1,
        grid=(B, nts),
        in_specs=[pl.BlockSpec((1, ROUTE_TILE, D), lambda b, i, bs: (b, i, 0)),
                  pl.BlockSpec((ROUTE_TILE, LANES), lambda b, i, bs: (b * nts + i, 0)),
                  pl.BlockSpec((1, 6, D), lambda b, i, bs: (b, 0, 0)),
                  pl.BlockSpec((1, D), lambda b, i, bs: (0, 0)),
                  pl.BlockSpec((LANES, N_EXPERTS * CHUNK), lambda b, i, bs: (0, 0)),
                  pl.BlockSpec(memory_space=pl.ANY)],
        out_specs=pl.BlockSpec((1, ROUTE_TILE, D), lambda b, i, bs: (b, i, 0)),
        scratch_shapes=[pltpu.VMEM((2, N_EXPERTS * CHUNK, D), BF16),
                        pltpu.VMEM((N_EXPERTS, CHUNK, D), BF16),
                        pltpu.VMEM((ROUTE_TILE, D), F32),
                        pltpu.SemaphoreType.DMA((2, N_EXPERTS)),
                        pltpu.SemaphoreType.DMA((N_EXPERTS,))])
    return pl.pallas_call(
        functools.partial(_combine_kernel, cap=cap),
        grid_spec=grid_spec,
        out_shape=jax.ShapeDtypeStruct((B, S, D), F32),
        compiler_params=_cparams(("arbitrary", "arbitrary")),
        name="combine",
    )(bs, x1, post, ada, norm_w.reshape(1, D), expand, ys)


def _prepare_weights(w_in, w_branch_attn, w_branch_gla, w_out, w_router, w_gate_e, w_up_e, w_down_e):
    aq, ak, av, gq, gk, gv, gr, glf, glb, ga, gg = jnp.split(
        w_in, np.cumsum([512, 128, 128, 256, 256, 512, 512, 16, 16, 1024]).tolist(), axis=1)
    pad = jnp.zeros((D, NP - C_GL - 2 * GLA_RANK), F32)
    w_in_p = jnp.concatenate([aq * LOG2E, gv, 0.5 * ga, 0.5 * gg, gr, ak, av, gq, gk, glf, glb, pad],
                             axis=1).astype(BF16)
    wr = jnp.zeros((D, LANES), F32).at[:, 0:N_EXPERTS].set(w_router)
    wr_hi = wr.astype(BF16)
    wr_cat = jnp.concatenate([wr_hi, (wr - wr_hi.astype(F32)).astype(BF16)], axis=1)
    expand = np.zeros((LANES, N_EXPERTS * CHUNK), np.float32)
    for e in range(N_EXPERTS):
        expand[e, e * CHUNK:(e + 1) * CHUNK] = 1.0
    return dict(w_in_p=w_in_p, w_ba=w_branch_attn.astype(BF16), w_bg=w_branch_gla.astype(BF16),
                w_out=(0.5 * w_out).astype(BF16), wr=wr_cat, w_gate=w_gate_e,
                w_up=w_up_e, w_down=w_down_e, expand=jnp.asarray(expand, BF16),
                bias=_attn_bias_table())


def _layer(x, ada, w, p):
    B, S, _ = x.shape
    n_tok = B * S
    cap = CAPACITY_FACTOR * n_tok // N_EXPERTS
    assert n_tok % ROUTE_TILE == 0 and S % ROUTE_TILE == 0 and cap % CHUNK == 0
    proj = _inproj(x, ada, p['norm_pre_mix'], w['w_in_p'])
    attn = _attention(proj, p['attn_sink'], w['bias'])
    o_g = _gla(proj, p['gla_wa2_fwd'], p['gla_ba_fwd'], p['gla_wa2_bwd'], p['gla_ba_bwd'], p['gla_norm'])
    x1, h2, g3, aff_t = _post(x, attn, o_g, proj, ada, w['w_ba'], w['w_bg'], w['w_out'],
                              p['norm_post_mix'], p['norm_pre_ffn'], w['wr'])
    selpos, post, bs = _select(aff_t, cap)
    xs = _compact(bs, h2.reshape(n_tok, D), g3.reshape(n_tok, LANES), selpos, cap)
    ys = _ffn(xs, w['w_gate'], w['w_up'], w['w_down'], cap)
    return _combine(bs, x1, post, ada, p['norm_post_ffn'], w['expand'], ys, cap)


def kernel(x_prompt, x_sample, c_prompt, c_sample, w_ada, b_ada, norm_pre_mix, norm_post_mix, w_in, attn_sink, gla_wa2_fwd, gla_ba_fwd, gla_wa2_bwd, gla_ba_bwd, gla_norm, w_branch_attn, w_branch_gla, w_out, norm_pre_ffn, norm_post_ffn, w_router, w_gate_e, w_up_e, w_down_e):
    assert w_ada.shape[0] == 1, "one layer"
    p = dict(norm_pre_mix=norm_pre_mix[0], norm_post_mix=norm_post_mix[0], attn_sink=attn_sink[0],
             gla_wa2_fwd=gla_wa2_fwd[0], gla_ba_fwd=gla_ba_fwd[0], gla_wa2_bwd=gla_wa2_bwd[0],
             gla_ba_bwd=gla_ba_bwd[0], gla_norm=gla_norm[0], norm_pre_ffn=norm_pre_ffn[0],
             norm_post_ffn=norm_post_ffn[0])
    w = _prepare_weights(w_in[0], w_branch_attn[0], w_branch_gla[0], w_out[0], w_router[0],
                         w_gate_e[0], w_up_e[0], w_down_e[0])
    bp, bs_ = c_prompt.shape[0], c_sample.shape[0]
    rows = -(-(bp + bs_) // 8) * 8
    c_all = jnp.concatenate([c_prompt, c_sample, jnp.zeros((rows - bp - bs_, D), F32)], axis=0)
    ada = _ada(c_all, w_ada[0], b_ada[0]).reshape(rows, 6, D)
    y_prompt = _layer(x_prompt, ada[0:bp], w, p)
    y_sample = _layer(x_sample, ada[bp:bp + bs_], w, p)
    return (y_prompt, y_sample)
```

```python
import functools

import numpy as np
import jax
import jax.numpy as jnp
from jax import lax
from jax.experimental import pallas as pl
from jax.experimental.pallas import tpu as pltpu

F32, BF16, I32 = jnp.float32, jnp.bfloat16, jnp.int32

D = 1024
N_Q_HEADS, N_KV_HEADS, HEAD_DIM = 8, 2, 64
ATTN_BLOCK = 128
GLA_HEADS, GLA_K, GLA_V = 4, 64, 128
GLA_RANK = 16
GLA_NORMALIZER = 16.0
N_EXPERTS, FF = 16, 1024
CAPACITY_FACTOR = 2
EPS = 1e-6
NEG = -1e30
LOG2E = 1.4426950408889634

C_AQ, C_GV, C_GA, C_GG, C_GR, C_KV, C_GQ, C_GK, C_GL = 0, 512, 1024, 2048, 3072, 3584, 3840, 4096, 4352
NP = 4480

LANES = 128
ROW_ALIGN = 16
CHUNK = 64
ROUTE_TILE = 256
XS_W = D + LANES
VMEM_LIMIT = 56 * 1024 * 1024


def _cparams(sem):
    return pltpu.CompilerParams(dimension_semantics=sem, vmem_limit_bytes=VMEM_LIMIT)


def _dot(a, b):
    return jnp.dot(a, b, preferred_element_type=F32)


def _dot_nt(a, b):
    return lax.dot_general(a, b, (((1,), (1,)), ((), ())), preferred_element_type=F32)


def _dot_tn(a, b):
    return lax.dot_general(a, b, (((0,), (0,)), ((), ())), preferred_element_type=F32)


def _split2(x):
    hi = x.astype(BF16)
    lo = (x - hi.astype(F32)).astype(BF16)
    return hi, lo


def _split3(x):
    hi = x.astype(BF16)
    r = x - hi.astype(F32)
    mid = r.astype(BF16)
    lo = (r - mid.astype(F32)).astype(BF16)
    return hi, mid, lo


def _sigmoid(x):
    return 0.5 * jnp.tanh(0.5 * x) + 0.5


def _rms(x, w):
    return x * lax.rsqrt(jnp.mean(x * x, axis=-1, keepdims=True) + EPS) * w


def _ada_kernel(c_ref, w_ref, b_ref, o_ref):
    c = c_ref[...]
    s = c * jax.nn.sigmoid(c)
    s_hi, s_lo = _split2(s)
    w_hi, w_lo = _split2(w_ref[...])
    o_ref[...] = _dot(s_hi, w_hi) + _dot(s_lo, w_hi) + _dot(s_hi, w_lo) + b_ref[...]


def _ada(c, w_ada, b_ada):
    rows = c.shape[0]
    tn = 1024
    return pl.pallas_call(
        _ada_kernel,
        grid=(6 * D // tn,),
        in_specs=[pl.BlockSpec((rows, D), lambda j: (0, 0)),
                  pl.BlockSpec((D, tn), lambda j: (0, j)),
                  pl.BlockSpec((1, tn), lambda j: (0, j))],
        out_specs=pl.BlockSpec((rows, tn), lambda j: (0, j)),
        out_shape=jax.ShapeDtypeStruct((rows, 6 * D), F32),
        compiler_params=_cparams(("arbitrary",)),
        name="ada",
    )(c, w_ada, b_ada.reshape(1, 6 * D))


def _inproj_kernel(x_ref, ada_ref, nw_ref, w_ref, o_ref):
    x = x_ref[0]
    h = (_rms(x, nw_ref[...] * (1.0 + ada_ref[0, 1:2, :])) + ada_ref[0, 0:1, :]).astype(BF16)
    step = 1024
    for c0 in range(0, NP, step):
        c1 = min(c0 + step, NP)
        o_ref[0, :, c0:c1] = _dot(h, w_ref[:, c0:c1]).astype(BF16)


def _inproj(x, ada, norm_w, w_in_p):
    B, S, _ = x.shape
    tm = 1024
    return pl.pallas_call(
        _inproj_kernel,
        grid=(B, S // tm),
        in_specs=[pl.BlockSpec((1, tm, D), lambda b, i: (b, i, 0)),
                  pl.BlockSpec((1, 6, D), lambda b, i: (b, 0, 0)),
                  pl.BlockSpec((1, D), lambda b, i: (0, 0)),
                  pl.BlockSpec((D, NP), lambda b, i: (0, 0), pipeline_mode=pl.Buffered(1))],
        out_specs=pl.BlockSpec((1, tm, NP), lambda b, i: (b, i, 0)),
        out_shape=jax.ShapeDtypeStruct((B, S, NP), BF16),
        compiler_params=_cparams(("arbitrary", "arbitrary")),
        name="inproj",
    )(x, ada, norm_w.reshape(1, D), w_in_p)


ATTN_TQ = 2048
ATTN_QB = ATTN_TQ // ATTN_BLOCK


_ATTN_STACK = (0, 2, 1, 3)


def _attn_bias_table():
    L = ATTN_BLOCK
    group = N_Q_HEADS // N_KV_HEADS
    q = np.arange(L)[:, None]
    k = np.arange(3 * L)[None, :]
    dist = np.abs(q - k + L).astype(np.float32)
    slopes = np.asarray([2.0 ** (-8.0 * (h + 1) / N_Q_HEADS) for h in range(N_Q_HEADS)], np.float32)
    base = -slopes[:, None, None] * dist[None] * np.float32(LOG2E)
    inside = (dist <= L)[None]
    variants = []
    for ok_k in (np.ones_like(k, bool), k >= L, k < 2 * L):
        per_head = np.where(inside & ok_k[None], base, np.float32(NEG))
        variants.append(np.stack([np.concatenate([per_head[hkv * group + g] for g in _ATTN_STACK], axis=0)
                                  for hkv in range(N_KV_HEADS)]))
    return jnp.asarray(np.stack(variants), F32)


def _attn_kernel(sink_ref, q_ref, kvm_ref, kvp_ref, kvn_ref, bias_ref, o_ref):
    L = ATTN_BLOCK
    i = pl.program_id(1)
    last = pl.num_programs(1) - 1
    kv = jnp.concatenate([kvp_ref[0], kvm_ref[0], kvn_ref[0]], axis=0)
    k = kv[:, :LANES] * jnp.asarray(HEAD_DIM ** -0.5, BF16)
    v = kv[:, LANES:]
    lo = lax.broadcasted_iota(I32, (1, LANES), 1) < HEAD_DIM
    zero = jnp.zeros_like(k)

    def halves(t):
        sw = jnp.concatenate([t[:, HEAD_DIM:], t[:, :HEAD_DIM]], axis=1)
        return ((jnp.where(lo, t, zero), jnp.where(lo, zero, sw)),
                (jnp.where(lo, sw, zero), jnp.where(lo, zero, t)))

    kh, vh = halves(k), halves(v)
    group = N_Q_HEADS // N_KV_HEADS
    rowblk = lax.broadcasted_iota(I32, (group * L, 1), 0) // L
    for j in range(ATTN_QB):
        first = jnp.logical_and(i == 0, j == 0)
        final = jnp.logical_and(i == last, j == ATTN_QB - 1)
        var = jnp.where(first, 1, jnp.where(final, 2, 0))
        r0 = j * L
        for hkv in range(N_KV_HEADS):
            c0 = hkv * 2 * LANES
            qs = jnp.concatenate([q_ref[0, r0:r0 + L, c0:c0 + LANES],
                                  q_ref[0, r0:r0 + L, c0 + LANES:c0 + 2 * LANES]], axis=0)
            s = jnp.concatenate([_dot_nt(qs, kh[hkv][0][r0:r0 + 3 * L]),
                                 _dot_nt(qs, kh[hkv][1][r0:r0 + 3 * L])], axis=0) + bias_ref[var, hkv]
            snk = jnp.zeros((group * L, 1), F32)
            for b, g in enumerate(_ATTN_STACK):
                snk = jnp.where(rowblk == b, sink_ref[hkv * group + g], snk)
            mx = jnp.maximum(jnp.max(s, axis=-1, keepdims=True), snk)
            p = jnp.exp2(s - mx)
            rden = 1.0 / (jnp.sum(p, axis=-1, keepdims=True) + jnp.exp2(snk - mx))
            pb = p.astype(BF16)
            o = (_dot(pb[0:2 * L], vh[hkv][0][r0:r0 + 3 * L]) * rden[0:2 * L]
                 + _dot(pb[2 * L:4 * L], vh[hkv][1][r0:r0 + 3 * L]) * rden[2 * L:4 * L])
            o_ref[0, r0:r0 + L, c0:c0 + LANES] = o[0:L].astype(BF16)
            o_ref[0, r0:r0 + L, c0 + LANES:c0 + 2 * LANES] = o[L:2 * L].astype(BF16)


def _attention(proj, sink, bias):
    B, S, _ = proj.shape
    L = ATTN_BLOCK
    nb = S // L
    assert S % ATTN_TQ == 0 and nb >= 2
    kvc = C_KV // 256
    return pl.pallas_call(
        _attn_kernel,
        grid=(B, S // ATTN_TQ),
        in_specs=[pl.BlockSpec(memory_space=pltpu.SMEM),
                  pl.BlockSpec((1, ATTN_TQ, 512), lambda b, i: (b, i, C_AQ // 512)),
                  pl.BlockSpec((1, ATTN_TQ, 256), lambda b, i: (b, i, kvc)),
                  pl.BlockSpec((1, L, 256), lambda b, i: (b, jnp.maximum(i * ATTN_QB - 1, 0), kvc)),
                  pl.BlockSpec((1, L, 256), lambda b, i: (b, jnp.minimum(i * ATTN_QB + ATTN_QB, nb - 1), kvc)),
                  pl.BlockSpec((3, N_KV_HEADS, (N_Q_HEADS // N_KV_HEADS) * L, 3 * L), lambda b, i: (0, 0, 0, 0))],
        out_specs=pl.BlockSpec((1, ATTN_TQ, 512), lambda b, i: (b, i, 0)),
        out_shape=jax.ShapeDtypeStruct((B, S, 512), BF16),
        compiler_params=_cparams(("arbitrary", "arbitrary")),
        name="attn",
    )(sink * LOG2E, proj, proj, proj, proj, bias)


GLA_TS = 2048
GLA_C = 128
GLA_NC = GLA_TS // GLA_C


def _gla_tile(q_ref, k_ref, v_ref, gl_ref, w_ref, ba_ref, st_ref, rev):
    C = GLA_C
    x = _dot(gl_ref[0], w_ref[...]) + ba_ref[...]
    la = (jnp.minimum(x, 0.0) - jnp.log(1.0 + jnp.exp(-jnp.abs(x)))) * (1.0 / GLA_NORMALIZER)
    row = lax.broadcasted_iota(I32, (C, C), 0)
    col = lax.broadcasted_iota(I32, (C, C), 1)
    if rev:
        tri = jnp.where(col >= row, 1.0, 0.0).astype(BF16)
        keep = col > row
    else:
        tri = jnp.where(col <= row, 1.0, 0.0).astype(BF16)
        keep = col <= row
    hi, mid, lo3 = _split3(la)
    bcs, halves, ehs = [], [], []
    for c in range(GLA_NC):
        rs = slice(c * C, (c + 1) * C)
        bc = _dot(tri, hi[rs]) + _dot(tri, mid[rs]) + _dot(tri, lo3[rs])
        half = 0.5 * (bc[0:1] if rev else bc[C - 1:C])
        bcs.append(bc)
        halves.append(jnp.broadcast_to(half, bc.shape))
        ehs.append(jnp.exp(half))
    bc = jnp.concatenate(bcs, axis=0)
    half = jnp.concatenate(halves, axis=0)
    eh = jnp.concatenate([jnp.broadcast_to(e, (C, e.shape[1])) for e in ehs], axis=0)
    qe = q_ref[0].astype(F32) * (GLA_K ** -0.5) * jnp.exp(bc - half)
    ke = k_ref[0].astype(F32) * jnp.exp(half - bc)
    qt, kt = qe.astype(BF16), ke.astype(BF16)
    qh = (qe * eh).astype(BF16)
    kb = (ke * eh).astype(BF16)
    lo = lax.broadcasted_iota(I32, (1, LANES), 1) < GLA_K
    zero = jnp.zeros((GLA_TS, LANES), BF16)
    order = range(GLA_NC - 1, -1, -1) if rev else range(GLA_NC)
    outs = [[None] * GLA_HEADS for _ in range(GLA_NC)]
    for m in range(GLA_HEADS // 2):
        sl = slice(m * LANES, (m + 1) * LANES)
        ktm = (jnp.where(lo, kt[:, sl], zero), jnp.where(lo, zero, kt[:, sl]))
        qhm = (jnp.where(lo, qh[:, sl], zero), jnp.where(lo, zero, qh[:, sl]))
        st = st_ref[m]
        for c in order:
            rs = slice(c * C, (c + 1) * C)
            stb = st.astype(BF16)
            upd = []
            for par in range(2):
                h = 2 * m + par
                a = jnp.where(keep, _dot_nt(qt[rs, sl], ktm[par][rs]), 0.0).astype(BF16)
                vh = v_ref[0, rs, h * GLA_V:(h + 1) * GLA_V]
                outs[c][h] = _dot(a, vh) + _dot_nt(qhm[par][rs], stb)
                upd.append(_dot_tn(vh, kb[rs, sl]))
            dec = ehs[c][:, sl] * ehs[c][:, sl]
            st = st * dec + jnp.where(lo, upd[0], upd[1])
        st_ref[m] = st
    return outs


def _gla_fwd_kernel(q_ref, k_ref, v_ref, gl_ref, w_ref, ba_ref, o_ref, st_ref):
    @pl.when(pl.program_id(1) == 0)
    def _():
        st_ref[...] = jnp.zeros_like(st_ref)

    outs = _gla_tile(q_ref, k_ref, v_ref, gl_ref, w_ref, ba_ref, st_ref, rev=False)
    for c in range(GLA_NC):
        for h in range(GLA_HEADS):
            o_ref[0, c * GLA_C:(c + 1) * GLA_C, h * GLA_V:(h + 1) * GLA_V] = outs[c][h].astype(BF16)


def _gla_bwd_kernel(q_ref, k_ref, v_ref, gl_ref, w_ref, ba_ref, of_ref, gr_ref, nw_ref, o_ref, st_ref):
    @pl.when(pl.program_id(1) == 0)
    def _():
        st_ref[...] = jnp.zeros_like(st_ref)

    outs = _gla_tile(q_ref, k_ref, v_ref, gl_ref, w_ref, ba_ref, st_ref, rev=True)
    for h in range(GLA_HEADS):
        hs = slice(h * GLA_V, (h + 1) * GLA_V)
        o = jnp.concatenate([outs[c][h] for c in range(GLA_NC)], axis=0) + of_ref[0, :, hs].astype(F32)
        g = gr_ref[0, :, hs].astype(F32)
        o_ref[0, :, hs] = (_rms(o, nw_ref[...]) * (g * _sigmoid(g))).astype(BF16)


def _gla(proj, wa2_f, ba_f, wa2_b, ba_b, gla_norm):
    B, S, _ = proj.shape
    nt = S // GLA_TS
    assert S % GLA_TS == 0
    kw = GLA_HEADS * GLA_K
    wf = jnp.zeros((LANES, kw), F32).at[0:GLA_RANK].set(wa2_f).astype(BF16)
    wb = jnp.zeros((LANES, kw), F32).at[GLA_RANK:2 * GLA_RANK].set(wa2_b).astype(BF16)

    def specs(tile):
        return [pl.BlockSpec((1, GLA_TS, 256), lambda b, i: (b, tile(i), C_GQ // 256)),
                pl.BlockSpec((1, GLA_TS, 256), lambda b, i: (b, tile(i), C_GK // 256)),
                pl.BlockSpec((1, GLA_TS, 512), lambda b, i: (b, tile(i), C_GV // 512)),
                pl.BlockSpec((1, GLA_TS, LANES), lambda b, i: (b, tile(i), C_GL // LANES)),
                pl.BlockSpec((LANES, kw), lambda b, i: (0, 0)),
                pl.BlockSpec((1, kw), lambda b, i: (0, 0))]

    scratch = [pltpu.VMEM((GLA_HEADS // 2, GLA_V, LANES), F32)]
    fwd_tile = lambda i: i
    o_f = pl.pallas_call(
        _gla_fwd_kernel,
        grid=(B, nt),
        in_specs=specs(fwd_tile),
        out_specs=pl.BlockSpec((1, GLA_TS, 512), lambda b, i: (b, i, 0)),
        out_shape=jax.ShapeDtypeStruct((B, S, 512), BF16),
        scratch_shapes=scratch,
        compiler_params=_cparams(("arbitrary", "arbitrary")),
        name="gla_fwd",
    )(proj, proj, proj, proj, wf, ba_f.reshape(1, kw))
    bwd_tile = lambda i: nt - 1 - i
    return pl.pallas_call(
        _gla_bwd_kernel,
        grid=(B, nt),
        in_specs=specs(bwd_tile) + [
            pl.BlockSpec((1, GLA_TS, 512), lambda b, i: (b, bwd_tile(i), 0)),
            pl.BlockSpec((1, GLA_TS, 512), lambda b, i: (b, bwd_tile(i), C_GR // 512)),
            pl.BlockSpec((1, GLA_V), lambda b, i: (0, 0))],
        out_specs=pl.BlockSpec((1, GLA_TS, 512), lambda b, i: (b, bwd_tile(i), 0)),
        out_shape=jax.ShapeDtypeStruct((B, S, 512), BF16),
        scratch_shapes=scratch,
        compiler_params=_cparams(("arbitrary", "arbitrary")),
        name="gla_bwd",
    )(proj, proj, proj, proj, wb, ba_b.reshape(1, kw), o_f, proj, gla_norm.reshape(1, GLA_V))


POST_SUB = 256

def _post_kernel(x_ref, attn_ref, og_ref, ga_ref, gg_ref, ada_ref, wba_ref, wbg_ref, wo_ref, npost_ref,
                 npre_ref, wr_ref, x1_ref, h2_ref, g3_ref, afft_ref):
    w_post = ada_ref[0, 2:3, :] * npost_ref[...]
    w_pre = (1.0 + ada_ref[0, 4:5, :]) * npre_ref[...]
    shift = ada_ref[0, 3:4, :]
    for r0 in range(0, x_ref.shape[1], POST_SUB):
        rs = slice(r0, r0 + POST_SUB)
        ga = ga_ref[0, rs, :].astype(F32)
        gg = gg_ref[0, rs, :].astype(F32)
        merged = ((jnp.tanh(ga) + 1.0) * _dot(attn_ref[0, rs, :], wba_ref[...])
                  + (jnp.tanh(gg) + 1.0) * _dot(og_ref[0, rs, :], wbg_ref[...]))
        mix = _dot(merged.astype(BF16), wo_ref[...])
        x1 = x_ref[0, rs, :] + _rms(mix, w_post)
        x1_ref[0, rs, :] = x1
        h2 = _rms(x1, w_pre) + shift
        h2_ref[0, rs, :] = h2.astype(BF16)
        h_hi, h_lo = _split2(h2)
        r = _dot(h_hi, wr_ref[...]) + _dot(h_lo, wr_ref[...])
        logits = r[:, 0:LANES] + r[:, LANES:2 * LANES]
        lane = lax.broadcasted_iota(I32, logits.shape, 1)
        logits = jnp.where(lane < N_EXPERTS, logits, NEG)
        e = jnp.exp(logits - jnp.max(logits, axis=-1, keepdims=True))
        aff = e / jnp.sum(e, axis=-1, keepdims=True)
        a_hi, a_mid, a_lo = _split3(aff)
        g3 = (a_hi.astype(F32) + pltpu.roll(a_mid.astype(F32), N_EXPERTS, 1)
              + pltpu.roll(a_lo.astype(F32), 2 * N_EXPERTS, 1))
        g3_ref[0, rs, :] = g3.astype(BF16)
        afft_ref[:, rs] = aff.T[0:N_EXPERTS, :]


def _post(x, attn, o_g, proj, ada, w_ba, w_bg, w_out, n_post, n_pre, wr):
    B, S, _ = x.shape
    tm = 1024
    nt = S // tm
    const = lambda b, i: (0, 0)
    return pl.pallas_call(
        _post_kernel,
        grid=(B, nt),
        in_specs=[pl.BlockSpec((1, tm, D), lambda b, i: (b, i, 0)),
                  pl.BlockSpec((1, tm, 512), lambda b, i: (b, i, 0)),
                  pl.BlockSpec((1, tm, 512), lambda b, i: (b, i, 0)),
                  pl.BlockSpec((1, tm, D), lambda b, i: (b, i, C_GA // D)),
                  pl.BlockSpec((1, tm, D), lambda b, i: (b, i, C_GG // D)),
                  pl.BlockSpec((1, 6, D), lambda b, i: (b, 0, 0)),
                  pl.BlockSpec((512, D), const),
                  pl.BlockSpec((512, D), const),
                  pl.BlockSpec((D, D), const),
                  pl.BlockSpec((1, D), const),
                  pl.BlockSpec((1, D), const),
                  pl.BlockSpec((D, 2 * LANES), const)],
        out_specs=[pl.BlockSpec((1, tm, D), lambda b, i: (b, i, 0)),
                   pl.BlockSpec((1, tm, D), lambda b, i: (b, i, 0)),
                   pl.BlockSpec((1, tm, LANES), lambda b, i: (b, i, 0)),
                   pl.BlockSpec((N_EXPERTS, tm), lambda b, i: (0, b * nt + i))],
        out_shape=[jax.ShapeDtypeStruct((B, S, D), F32),
                   jax.ShapeDtypeStruct((B, S, D), BF16),
                   jax.ShapeDtypeStruct((B, S, LANES), BF16),
                   jax.ShapeDtypeStruct((N_EXPERTS, B * S), F32)],
        compiler_params=_cparams(("arbitrary", "arbitrary")),
        name="post",
    )(x, attn, o_g, proj, proj, ada, w_ba, w_bg, w_out, n_post.reshape(1, D), n_pre.reshape(1, D), wr)


def _select_kernel(aff_ref, selpos_ref, post_ref, bs_ref, *, cap, n_tok):
    nblk = n_tok // LANES

    def count(mask):
        return jnp.sum(jnp.where(mask, 1.0, 0.0), axis=1, keepdims=True).astype(I32)

    def bit_body(it, bits):
        cand = bits | jnp.left_shift(jnp.int32(1), 30 - it)
        return jnp.where(count(aff_ref[...] >= pltpu.bitcast(cand, F32)) >= cap, cand, bits)

    thr = pltpu.bitcast(lax.fori_loop(0, 31, bit_body, jnp.zeros((N_EXPERTS, 1), I32)), F32)
    need = cap - count(aff_ref[...] > thr)
    r = lax.broadcasted_iota(I32, (LANES, LANES), 0)
    c = lax.broadcasted_iota(I32, (LANES, LANES), 1)
    scan_mat = jnp.concatenate([jnp.where(r <= c, 1.0, 0.0), jnp.ones((LANES, LANES), F32)], axis=1).astype(BF16)
    blk_lane = lax.broadcasted_iota(I32, bs_ref.shape, 1)
    filler = jnp.full((LANES - N_EXPERTS, LANES), -1.0, F32)
    reps = bs_ref.shape[1] // LANES

    def taken_before(c_gt, c_eq):
        return jnp.concatenate([c_gt + jnp.minimum(c_eq, need)] * reps, axis=1)

    def blk_body(j, carry):
        c_gt, c_eq, bs = carry
        off = pl.multiple_of(j * LANES, LANES)
        a = aff_ref[:, pl.ds(off, LANES)]
        gt = a > thr
        eq = a == thr
        both = jnp.concatenate([jnp.where(gt, 1.0, 0.0), jnp.where(eq, 1.0, 0.0)], axis=0).astype(BF16)
        scan = _dot(both, scan_mat).astype(I32)
        ex_gt = c_gt + scan[0:N_EXPERTS, 0:LANES] - jnp.where(gt, 1, 0)
        ex_eq = c_eq + scan[N_EXPERTS:, 0:LANES] - jnp.where(eq, 1, 0)
        sel = jnp.logical_or(gt, jnp.logical_and(eq, ex_eq < need))
        pos = ex_gt + jnp.minimum(ex_eq, need)
        sp = jnp.where(sel, pos, -1)
        selpos_ref[:, pl.ds(off, LANES)] = sp
        post_ref[pl.ds(off, LANES), :] = jnp.concatenate([sp.astype(F32), filler], axis=0).T
        bs = jnp.where(blk_lane == j, taken_before(c_gt, c_eq), bs)
        return c_gt + scan[0:N_EXPERTS, LANES:], c_eq + scan[N_EXPERTS:, LANES:], bs

    z = jnp.zeros((N_EXPERTS, LANES), I32)
    c_gt, c_eq, bs = lax.fori_loop(0, nblk, blk_body, (z, z, jnp.zeros(bs_ref.shape, I32)), unroll=4)
    bs_ref[...] = jnp.where(blk_lane >= nblk, taken_before(c_gt, c_eq), bs)


def _select(aff_t, cap):
    n_tok = aff_t.shape[1]
    nbp = ((n_tok // LANES + 1 + LANES - 1) // LANES) * LANES
    return pl.pallas_call(
        functools.partial(_select_kernel, cap=cap, n_tok=n_tok),
        out_shape=[jax.ShapeDtypeStruct((N_EXPERTS, n_tok), I32),
                   jax.ShapeDtypeStruct((n_tok, LANES), F32),
                   jax.ShapeDtypeStruct((N_EXPERTS, nbp), I32)],
        compiler_params=pltpu.CompilerParams(vmem_limit_bytes=VMEM_LIMIT),
        name="select",
    )(aff_t)


def _row_window(bs_ref, e, tile):
    per = ROUTE_TILE // LANES
    start = bs_ref[e, tile * per]
    cnt = bs_ref[e, tile * per + per] - start
    base = (start // ROW_ALIGN) * ROW_ALIGN
    return start, cnt, base, start - base


def _compact_kernel(bs_ref, h2_ref, g3_ref, selpos_ref, xs_ref, stage_ref, over_ref, carry_ref, sems, osems, pend_ref):
    t = pl.program_id(0)
    nt = pl.num_programs(0)
    par = t % 2
    cap = xs_ref.shape[1] - CHUNK

    def strip_copy(slot, e, base):
        return pltpu.make_async_copy(stage_ref.at[slot, pl.ds(e * CHUNK, CHUNK), :],
                                     xs_ref.at[e, pl.ds(pl.multiple_of(base, ROW_ALIGN), CHUNK), :], sems.at[slot])

    def over_copy(e, base):
        return pltpu.make_async_copy(over_ref.at[e], xs_ref.at[e, pl.ds(pl.multiple_of(base, ROW_ALIGN), CHUNK), :],
                                     osems.at[e])

    @pl.when(t == 0)
    def _():
        over_ref[0] = jnp.zeros((CHUNK, XS_W), BF16)
        carry_ref[...] = jnp.zeros_like(carry_ref)
        for e in range(N_EXPERTS):
            pend_ref[e] = 0
            pltpu.make_async_copy(over_ref.at[0], xs_ref.at[e, pl.ds(cap, CHUNK), :], osems.at[e]).start()
        for e in range(N_EXPERTS):
            pltpu.make_async_copy(over_ref.at[0], xs_ref.at[e, pl.ds(cap, CHUNK), :], osems.at[e]).wait()

    jrow = lax.broadcasted_iota(I32, (CHUNK, ROUTE_TILE), 0)
    pieces = []
    for e in range(N_EXPERTS):
        _, _, base, _ = _row_window(bs_ref, e, t)
        pieces.append(jnp.where(selpos_ref[e:e + 1, :] - base == jrow, 1.0, 0.0).astype(BF16))
    onehot = jnp.concatenate(pieces, axis=0)
    stage_ref[par, :, 0:D] = _dot(onehot, h2_ref[...]).astype(BF16)
    stage_ref[par, :, D:XS_W] = _dot(onehot, g3_ref[...]).astype(BF16)

    @pl.when(t > 0)
    def _():
        for e in range(N_EXPERTS):
            strip_copy(1 - par, e, 0).wait()

    r16 = lax.broadcasted_iota(I32, (ROW_ALIGN, XS_W), 0)
    for e in range(N_EXPERTS):
        start, cnt, base, off = _row_window(bs_ref, e, t)
        head = pl.ds(e * CHUNK, ROW_ALIGN)
        stage_ref[par, head, :] = jnp.where(r16 < off, carry_ref[e], stage_ref[par, head, :])

        @pl.when(pend_ref[e] == 1)
        def _():
            over_copy(e, 0).wait()
            pend_ref[e] = 0

        strip_copy(par, e, base).start()
        end = start + cnt
        nch = (off + cnt + CHUNK - 1) // CHUNK
        gbase = (end // ROW_ALIGN) * ROW_ALIGN
        src = pl.multiple_of(e * CHUNK + jnp.minimum(gbase - base, CHUNK - ROW_ALIGN), ROW_ALIGN)
        carry_ref[e] = stage_ref[par, pl.ds(src, ROW_ALIGN), :]

        def more(cidx, _):
            cb = base + cidx * CHUNK

            @pl.when(pend_ref[e] == 1)
            def _():
                over_copy(e, 0).wait()

            oh = jnp.where(selpos_ref[e:e + 1, :] - cb == jrow, 1.0, 0.0).astype(BF16)
            over_ref[e, :, 0:D] = _dot(oh, h2_ref[...]).astype(BF16)
            over_ref[e, :, D:XS_W] = _dot(oh, g3_ref[...]).astype(BF16)
            over_copy(e, cb).start()
            pend_ref[e] = 1

            @pl.when(jnp.logical_and(cidx == nch - 1, gbase - cb < CHUNK))
            def _():
                carry_ref[e] = over_ref[e, pl.ds(pl.multiple_of(gbase - cb, ROW_ALIGN), ROW_ALIGN), :]
            return 0

        lax.fori_loop(1, nch, more, 0)

    @pl.when(t == nt - 1)
    def _():
        for e in range(N_EXPERTS):
            strip_copy(par, e, 0).wait()

            @pl.when(pend_ref[e] == 1)
            def _():
                over_copy(e, 0).wait()
                pend_ref[e] = 0


def _compact(bs, h2, g3, selpos, cap):
    n_tok = h2.shape[0]
    nt = n_tok // ROUTE_TILE
    grid_spec = pltpu.PrefetchScalarGridSpec(
        num_scalar_prefetch=1,
        grid=(nt,),
        in_specs=[pl.BlockSpec((ROUTE_TILE, D), lambda t, bs: (t, 0)),
                  pl.BlockSpec((ROUTE_TILE, LANES), lambda t, bs: (t, 0)),
                  pl.BlockSpec((N_EXPERTS, ROUTE_TILE), lambda t, bs: (0, t))],
        out_specs=pl.BlockSpec(memory_space=pl.ANY),
        scratch_shapes=[pltpu.VMEM((2, N_EXPERTS * CHUNK, XS_W), BF16),
                        pltpu.VMEM((N_EXPERTS, CHUNK, XS_W), BF16),
                        pltpu.VMEM((N_EXPERTS, ROW_ALIGN, XS_W), BF16),
                        pltpu.SemaphoreType.DMA((2,)),
                        pltpu.SemaphoreType.DMA((N_EXPERTS,)),
                        pltpu.SMEM((N_EXPERTS,), I32)])
    return pl.pallas_call(
        _compact_kernel,
        grid_spec=grid_spec,
        out_shape=jax.ShapeDtypeStruct((N_EXPERTS, cap + CHUNK, XS_W), BF16),
        compiler_params=_cparams(("arbitrary",)),
        name="compact",
    )(bs, h2, g3, selpos)


def _ffn_kernel(xs_ref, wg_ref, wu_ref, wd_ref, ys_ref, wgb_ref, wub_ref, wdb_ref):
    g = pl.program_id(0)
    j = pl.program_id(1)
    slab = wg_ref.shape[1]

    @pl.when(g < N_EXPERTS)
    def _():
        rows = pl.ds(pl.multiple_of(j * slab, slab), slab)
        wgb_ref[g % 2, rows, :] = wg_ref[0].astype(BF16)
        wub_ref[g % 2, rows, :] = wu_ref[0].astype(BF16)
        wdb_ref[g % 2, rows, :] = wd_ref[0].astype(BF16)

    @pl.when(g > 0)
    def _():
        e = g - 1
        cur = e % 2
        xs = xs_ref[0]
        x = xs[:, 0:D]
        lane = lax.broadcasted_iota(I32, (1, LANES), 1)
        mine = jnp.logical_or(lane == e, jnp.logical_or(lane == e + N_EXPERTS, lane == e + 2 * N_EXPERTS))
        gate = jnp.sum(jnp.where(mine, xs[:, D:XS_W].astype(F32), 0.0), axis=1, keepdims=True)
        a = _dot(x, wgb_ref[cur])
        hid = (a * _sigmoid(a) * _dot(x, wub_ref[cur])).astype(BF16)
        ys_ref[0] = (_dot(hid, wdb_ref[cur]) * gate).astype(BF16)


def _ffn(xs, w_gate, w_up, w_down, cap):
    tm = min(1024, cap)
    nj = cap // tm
    assert cap % tm == 0 and D % nj == 0 and FF % nj == 0
    last = N_EXPERTS - 1
    wmap = lambda g, j: (jnp.minimum(g, last), jnp.where(g <= last, j, nj - 1), 0)
    xmap = lambda g, j: (jnp.maximum(g - 1, 0), jnp.where(g == 0, 0, j), 0)
    return pl.pallas_call(
        _ffn_kernel,
        grid=(N_EXPERTS + 1, nj),
        in_specs=[pl.BlockSpec((1, tm, XS_W), xmap),
                  pl.BlockSpec((1, D // nj, FF), wmap),
                  pl.BlockSpec((1, D // nj, FF), wmap),
                  pl.BlockSpec((1, FF // nj, D), wmap)],
        out_specs=pl.BlockSpec((1, tm, D), xmap),
        out_shape=jax.ShapeDtypeStruct((N_EXPERTS, cap, D), BF16),
        scratch_shapes=[pltpu.VMEM((2, D, FF), BF16), pltpu.VMEM((2, D, FF), BF16), pltpu.VMEM((2, FF, D), BF16)],
        compiler_params=_cparams(("arbitrary", "arbitrary")),
        name="ffn",
    )(xs, w_gate, w_up, w_down)


def _combine_kernel(bs_ref, x1_ref, post_ref, ada_ref, nw_ref, expand_ref, ys_ref, o_ref,
                    strips_ref, over_ref, acc_ref, sems, osems, *, cap):
    nts = pl.num_programs(1)
    t = pl.program_id(0) * nts + pl.program_id(1)
    nt = pl.num_programs(0) * nts

    def strip_base(e, tile):
        _, _, base, _ = _row_window(bs_ref, e, tile)
        return jnp.minimum(base, cap - CHUNK)

    def strip_copy(e, tile, slot):
        src = ys_ref.at[e, pl.ds(pl.multiple_of(strip_base(e, tile), ROW_ALIGN), CHUNK), :]
        return pltpu.make_async_copy(src, strips_ref.at[slot, pl.ds(e * CHUNK, CHUNK), :], sems.at[slot])

    def over_copy(e, cb):
        return pltpu.make_async_copy(ys_ref.at[e, pl.ds(pl.multiple_of(cb, ROW_ALIGN), CHUNK), :], over_ref.at[e],
                                     osems.at[e])

    def n_strips(e):
        _, cnt, _, off = _row_window(bs_ref, e, t)
        return (off + cnt + CHUNK - 1) // CHUNK

    def over_base(e, cidx):
        _, _, base, _ = _row_window(bs_ref, e, t)
        return jnp.minimum(base + cidx * CHUNK, cap - CHUNK)

    @pl.when(t == 0)
    def _():
        for e in range(N_EXPERTS):
            strip_copy(e, 0, 0).start()

    @pl.when(t + 1 < nt)
    def _():
        for e in range(N_EXPERTS):
            strip_copy(e, t + 1, (t + 1) % 2).start()

    for e in range(N_EXPERTS):
        @pl.when(n_strips(e) >= 2)
        def _():
            over_copy(e, over_base(e, 1)).start()

    pos = post_ref[...]
    elane = lax.broadcasted_iota(I32, (1, LANES), 1)
    bases = jnp.zeros((1, LANES), I32)
    for e in range(N_EXPERTS):
        bases = jnp.where(elane == e, strip_base(e, t), bases)
    rel = pos - bases.astype(F32)
    inside = jnp.logical_and(pos >= 0.0, jnp.logical_and(rel >= 0.0, rel < float(CHUNK)))
    rel_x = _dot(jnp.where(inside, rel, -1.0).astype(BF16), expand_ref[...])
    want = (lax.broadcasted_iota(I32, (1, N_EXPERTS * CHUNK), 1) % CHUNK).astype(F32)
    onehot = jnp.where(rel_x == want, 1.0, 0.0).astype(BF16)

    slot = t % 2
    for e in range(N_EXPERTS):
        strip_copy(e, t, slot).wait()
    acc_ref[...] = _dot(onehot, strips_ref[slot])

    lane_c = lax.broadcasted_iota(I32, (1, CHUNK), 1)
    for e in range(N_EXPERTS):
        _, _, base, _ = _row_window(bs_ref, e, t)
        nch = n_strips(e)

        def more(cidx, _):
            cb = over_base(e, cidx)

            @pl.when(cidx >= 2)
            def _():
                over_copy(e, cb).start()

            tgt = (lane_c + cb).astype(F32)
            prev_end = (jnp.minimum(base + (cidx - 1) * CHUNK, cap - CHUNK) + CHUNK).astype(F32)
            col = post_ref[:, e:e + 1]
            oh = jnp.where(jnp.logical_and(col == tgt, col >= prev_end), 1.0, 0.0).astype(BF16)
            over_copy(e, cb).wait()
            acc_ref[...] += _dot(oh, over_ref[e])
            return 0

        lax.fori_loop(1, nch, more, 0)

    y = acc_ref[...]
    o_ref[0] = x1_ref[0] + _rms(y, ada_ref[0, 5:6, :] * nw_ref[...])


def _combine(bs, x1, post, ada, norm_w, expand, ys, cap):
    B, S, _ = x1.shape
    nts = S // ROUTE_TILE
    grid_spec = pltpu.PrefetchScalarGridSpec(
        num_scalar_prefetch=1,
        grid=(B, nts),
        in_specs=[pl.BlockSpec((1, ROUTE_TILE, D), lambda b, i, bs: (b, i, 0)),
                  pl.BlockSpec((ROUTE_TILE, LANES), lambda b, i, bs: (b * nts + i, 0)),
                  pl.BlockSpec((1, 6, D), lambda b, i, bs: (b, 0, 0)),
                  pl.BlockSpec((1, D), lambda b, i, bs: (0, 0)),
                  pl.BlockSpec((LANES, N_EXPERTS * CHUNK), lambda b, i, bs: (0, 0)),
                  pl.BlockSpec(memory_space=pl.ANY)],
        out_specs=pl.BlockSpec((1, ROUTE_TILE, D), lambda b, i, bs: (b, i, 0)),
        scratch_shapes=[pltpu.VMEM((2, N_EXPERTS * CHUNK, D), BF16),
                        pltpu.VMEM((N_EXPERTS, CHUNK, D), BF16),
                        pltpu.VMEM((ROUTE_TILE, D), F32),
                        pltpu.SemaphoreType.DMA((2,)),
                        pltpu.SemaphoreType.DMA((N_EXPERTS,))])
    return pl.pallas_call(
        functools.partial(_combine_kernel, cap=cap),
        grid_spec=grid_spec,
        out_shape=jax.ShapeDtypeStruct((B, S, D), F32),
        compiler_params=_cparams(("arbitrary", "arbitrary")),
        name="combine",
    )(bs, x1, post, ada, norm_w.reshape(1, D), expand, ys)


def _prepare_weights(w_in, w_branch_attn, w_branch_gla, w_out, w_router, w_gate_e, w_up_e, w_down_e):
    aq, ak, av, gq, gk, gv, gr, glf, glb, ga, gg = jnp.split(
        w_in, np.cumsum([512, 128, 128, 256, 256, 512, 512, 16, 16, 1024]).tolist(), axis=1)
    pad = jnp.zeros((D, NP - C_GL - 2 * GLA_RANK), F32)
    w_in_p = jnp.concatenate([aq * LOG2E, gv, 0.5 * ga, 0.5 * gg, gr, ak, av, gq, gk, glf, glb, pad],
                             axis=1).astype(BF16)
    wr = jnp.zeros((D, LANES), F32).at[:, 0:N_EXPERTS].set(w_router)
    wr_hi = wr.astype(BF16)
    wr_cat = jnp.concatenate([wr_hi, (wr - wr_hi.astype(F32)).astype(BF16)], axis=1)
    expand = np.zeros((LANES, N_EXPERTS * CHUNK), np.float32)
    for e in range(N_EXPERTS):
        expand[e, e * CHUNK:(e + 1) * CHUNK] = 1.0
    return dict(w_in_p=w_in_p, w_ba=w_branch_attn.astype(BF16), w_bg=w_branch_gla.astype(BF16),
                w_out=(0.5 * w_out).astype(BF16), wr=wr_cat, w_gate=w_gate_e,
                w_up=w_up_e, w_down=w_down_e, expand=jnp.asarray(expand, BF16),
                bias=_attn_bias_table())


def _layer(x, ada, w, p):
    B, S, _ = x.shape
    n_tok = B * S
    cap = CAPACITY_FACTOR * n_tok // N_EXPERTS
    assert n_tok % ROUTE_TILE == 0 and S % ROUTE_TILE == 0 and cap % CHUNK == 0
    proj = _inproj(x, ada, p['norm_pre_mix'], w['w_in_p'])
    attn = _attention(proj, p['attn_sink'], w['bias'])
    o_g = _gla(proj, p['gla_wa2_fwd'], p['gla_ba_fwd'], p['gla_wa2_bwd'], p['gla_ba_bwd'], p['gla_norm'])
    x1, h2, g3, aff_t = _post(x, attn, o_g, proj, ada, w['w_ba'], w['w_bg'], w['w_out'],
                              p['norm_post_mix'], p['norm_pre_ffn'], w['wr'])
    selpos, post, bs = _select(aff_t, cap)
    xs = _compact(bs, h2.reshape(n_tok, D), g3.reshape(n_tok, LANES), selpos, cap)
    ys = _ffn(xs, w['w_gate'], w['w_up'], w['w_down'], cap)
    return _combine(bs, x1, post, ada, p['norm_post_ffn'], w['expand'], ys, cap)


def kernel(x_prompt, x_sample, c_prompt, c_sample, w_ada, b_ada, norm_pre_mix, norm_post_mix, w_in, attn_sink, gla_wa2_fwd, gla_ba_fwd, gla_wa2_bwd, gla_ba_bwd, gla_norm, w_branch_attn, w_branch_gla, w_out, norm_pre_ffn, norm_post_ffn, w_router, w_gate_e, w_up_e, w_down_e):
    assert w_ada.shape[0] == 1, "one layer"
    p = dict(norm_pre_mix=norm_pre_mix[0], norm_post_mix=norm_post_mix[0], attn_sink=attn_sink[0],
             gla_wa2_fwd=gla_wa2_fwd[0], gla_ba_fwd=gla_ba_fwd[0], gla_wa2_bwd=gla_wa2_bwd[0],
             gla_ba_bwd=gla_ba_bwd[0], gla_norm=gla_norm[0], norm_pre_ffn=norm_pre_ffn[0],
             norm_post_ffn=norm_post_ffn[0])
    w = _prepare_weights(w_in[0], w_branch_attn[0], w_branch_gla[0], w_out[0], w_router[0],
                         w_gate_e[0], w_up_e[0], w_down_e[0])
    bp, bs_ = c_prompt.shape[0], c_sample.shape[0]
    rows = -(-(bp + bs_) // 8) * 8
    c_all = jnp.concatenate([c_prompt, c_sample, jnp.zeros((rows - bp - bs_, D), F32)], axis=0)
    ada = _ada(c_all, w_ada[0], b_ada[0]).reshape(rows, 6, D)
    y_prompt = _layer(x_prompt, ada[0:bp], w, p)
    y_sample = _layer(x_sample, ada[bp:bp + bs_], w, p)
    return (y_prompt, y_sample)
```

```python
import functools

import numpy as np
import jax
import jax.numpy as jnp
from jax import lax
from jax.experimental import pallas as pl
from jax.experimental.pallas import tpu as pltpu

F32, BF16, I32 = jnp.float32, jnp.bfloat16, jnp.int32

D = 1024
N_Q_HEADS, N_KV_HEADS, HEAD_DIM = 8, 2, 64
ATTN_BLOCK = 128
GLA_HEADS, GLA_K, GLA_V = 4, 64, 128
GLA_RANK = 16
GLA_NORMALIZER = 16.0
N_EXPERTS, FF = 16, 1024
CAPACITY_FACTOR = 2
EPS = 1e-6
NEG = -1e30
LOG2E = 1.4426950408889634

C_AQ, C_GV, C_GA, C_GG, C_GR, C_KV, C_GQ, C_GK, C_GL = 0, 512, 1024, 2048, 3072, 3584, 3840, 4096, 4352
NP = 4480

LANES = 128
ROW_ALIGN = 16
CHUNK = 64
ROUTE_TILE = 256
ROUTE_SUB = 2
XS_W = D + LANES
VMEM_LIMIT = 56 * 1024 * 1024


def _cparams(sem):
    return pltpu.CompilerParams(dimension_semantics=sem, vmem_limit_bytes=VMEM_LIMIT)


def _dot(a, b):
    return jnp.dot(a, b, preferred_element_type=F32)


def _dot_nt(a, b):
    return lax.dot_general(a, b, (((1,), (1,)), ((), ())), preferred_element_type=F32)


def _dot_tn(a, b):
    return lax.dot_general(a, b, (((0,), (0,)), ((), ())), preferred_element_type=F32)


def _split2(x):
    hi = x.astype(BF16)
    lo = (x - hi.astype(F32)).astype(BF16)
    return hi, lo


def _split3(x):
    hi = x.astype(BF16)
    r = x - hi.astype(F32)
    mid = r.astype(BF16)
    lo = (r - mid.astype(F32)).astype(BF16)
    return hi, mid, lo


def _sigmoid(x):
    return 0.5 * jnp.tanh(0.5 * x) + 0.5


def _rms(x, w):
    return x * lax.rsqrt(jnp.mean(x * x, axis=-1, keepdims=True) + EPS) * w


def _ada_kernel(c_ref, w_ref, b_ref, o_ref):
    c = c_ref[...]
    s = c * jax.nn.sigmoid(c)
    s_hi, s_lo = _split2(s)
    w_hi, w_lo = _split2(w_ref[...])
    o_ref[...] = _dot(s_hi, w_hi) + _dot(s_lo, w_hi) + _dot(s_hi, w_lo) + b_ref[...]


def _ada(c, w_ada, b_ada):
    rows = c.shape[0]
    tn = 1024
    return pl.pallas_call(
        _ada_kernel,
        grid=(6 * D // tn,),
        in_specs=[pl.BlockSpec((rows, D), lambda j: (0, 0)),
                  pl.BlockSpec((D, tn), lambda j: (0, j)),
                  pl.BlockSpec((1, tn), lambda j: (0, j))],
        out_specs=pl.BlockSpec((rows, tn), lambda j: (0, j)),
        out_shape=jax.ShapeDtypeStruct((rows, 6 * D), F32),
        compiler_params=_cparams(("arbitrary",)),
        name="ada",
    )(c, w_ada, b_ada.reshape(1, 6 * D))


def _inproj_kernel(x_ref, ada_ref, nw_ref, w_ref, o_ref):
    x = x_ref[0]
    h = (_rms(x, nw_ref[...] * (1.0 + ada_ref[0, 1:2, :])) + ada_ref[0, 0:1, :]).astype(BF16)
    step = 1024
    for c0 in range(0, NP, step):
        c1 = min(c0 + step, NP)
        o_ref[0, :, c0:c1] = _dot(h, w_ref[:, c0:c1]).astype(BF16)


def _inproj(x, ada, norm_w, w_in_p):
    B, S, _ = x.shape
    tm = 1024
    return pl.pallas_call(
        _inproj_kernel,
        grid=(B, S // tm),
        in_specs=[pl.BlockSpec((1, tm, D), lambda b, i: (b, i, 0)),
                  pl.BlockSpec((1, 6, D), lambda b, i: (b, 0, 0)),
                  pl.BlockSpec((1, D), lambda b, i: (0, 0)),
                  pl.BlockSpec((D, NP), lambda b, i: (0, 0), pipeline_mode=pl.Buffered(1))],
        out_specs=pl.BlockSpec((1, tm, NP), lambda b, i: (b, i, 0)),
        out_shape=jax.ShapeDtypeStruct((B, S, NP), BF16),
        compiler_params=_cparams(("arbitrary", "arbitrary")),
        name="inproj",
    )(x, ada, norm_w.reshape(1, D), w_in_p)


ATTN_TQ = 2048
ATTN_QB = ATTN_TQ // ATTN_BLOCK


_ATTN_STACK = (0, 2, 1, 3)


def _attn_bias_table():
    L = ATTN_BLOCK
    group = N_Q_HEADS // N_KV_HEADS
    q = np.arange(L)[:, None]
    k = np.arange(3 * L)[None, :]
    dist = np.abs(q - k + L).astype(np.float32)
    slopes = np.asarray([2.0 ** (-8.0 * (h + 1) / N_Q_HEADS) for h in range(N_Q_HEADS)], np.float32)
    base = -slopes[:, None, None] * dist[None] * np.float32(LOG2E)
    inside = (dist <= L)[None]
    variants = []
    for ok_k in (np.ones_like(k, bool), k >= L, k < 2 * L):
        per_head = np.where(inside & ok_k[None], base, np.float32(NEG))
        variants.append(np.stack([np.concatenate([per_head[hkv * group + g] for g in _ATTN_STACK], axis=0)
                                  for hkv in range(N_KV_HEADS)]))
    return jnp.asarray(np.stack(variants), F32)


def _attn_kernel(sink_ref, q_ref, kvm_ref, kvp_ref, kvn_ref, bias_ref, o_ref):
    L = ATTN_BLOCK
    i = pl.program_id(1)
    last = pl.num_programs(1) - 1
    kv = jnp.concatenate([kvp_ref[0], kvm_ref[0], kvn_ref[0]], axis=0)
    k = kv[:, :LANES] * jnp.asarray(HEAD_DIM ** -0.5, BF16)
    v = kv[:, LANES:]
    lo = lax.broadcasted_iota(I32, (1, LANES), 1) < HEAD_DIM
    zero = jnp.zeros_like(k)

    def halves(t):
        sw = jnp.concatenate([t[:, HEAD_DIM:], t[:, :HEAD_DIM]], axis=1)
        return ((jnp.where(lo, t, zero), jnp.where(lo, zero, sw)),
                (jnp.where(lo, sw, zero), jnp.where(lo, zero, t)))

    kh, vh = halves(k), halves(v)
    group = N_Q_HEADS // N_KV_HEADS
    rowblk = lax.broadcasted_iota(I32, (group * L, 1), 0) // L
    for j in range(ATTN_QB):
        first = jnp.logical_and(i == 0, j == 0)
        final = jnp.logical_and(i == last, j == ATTN_QB - 1)
        var = jnp.where(first, 1, jnp.where(final, 2, 0))
        r0 = j * L
        for hkv in range(N_KV_HEADS):
            c0 = hkv * 2 * LANES
            qs = jnp.concatenate([q_ref[0, r0:r0 + L, c0:c0 + LANES],
                                  q_ref[0, r0:r0 + L, c0 + LANES:c0 + 2 * LANES]], axis=0)
            s = jnp.concatenate([_dot_nt(qs, kh[hkv][0][r0:r0 + 3 * L]),
                                 _dot_nt(qs, kh[hkv][1][r0:r0 + 3 * L])], axis=0) + bias_ref[var, hkv]
            snk = jnp.zeros((group * L, 1), F32)
            for b, g in enumerate(_ATTN_STACK):
                snk = jnp.where(rowblk == b, sink_ref[hkv * group + g], snk)
            mx = jnp.maximum(jnp.max(s, axis=-1, keepdims=True), snk)
            p = jnp.exp2(s - mx)
            rden = 1.0 / (jnp.sum(p, axis=-1, keepdims=True) + jnp.exp2(snk - mx))
            pb = p.astype(BF16)
            o = (_dot(pb[0:2 * L], vh[hkv][0][r0:r0 + 3 * L]) * rden[0:2 * L]
                 + _dot(pb[2 * L:4 * L], vh[hkv][1][r0:r0 + 3 * L]) * rden[2 * L:4 * L])
            o_ref[0, r0:r0 + L, c0:c0 + LANES] = o[0:L].astype(BF16)
            o_ref[0, r0:r0 + L, c0 + LANES:c0 + 2 * LANES] = o[L:2 * L].astype(BF16)


def _attention(proj, sink, bias):
    B, S, _ = proj.shape
    L = ATTN_BLOCK
    nb = S // L
    assert S % ATTN_TQ == 0 and nb >= 2
    kvc = C_KV // 256
    return pl.pallas_call(
        _attn_kernel,
        grid=(B, S // ATTN_TQ),
        in_specs=[pl.BlockSpec(memory_space=pltpu.SMEM),
                  pl.BlockSpec((1, ATTN_TQ, 512), lambda b, i: (b, i, C_AQ // 512)),
                  pl.BlockSpec((1, ATTN_TQ, 256), lambda b, i: (b, i, kvc)),
                  pl.BlockSpec((1, L, 256), lambda b, i: (b, jnp.maximum(i * ATTN_QB - 1, 0), kvc)),
                  pl.BlockSpec((1, L, 256), lambda b, i: (b, jnp.minimum(i * ATTN_QB + ATTN_QB, nb - 1), kvc)),
                  pl.BlockSpec((3, N_KV_HEADS, (N_Q_HEADS // N_KV_HEADS) * L, 3 * L), lambda b, i: (0, 0, 0, 0))],
        out_specs=pl.BlockSpec((1, ATTN_TQ, 512), lambda b, i: (b, i, 0)),
        out_shape=jax.ShapeDtypeStruct((B, S, 512), BF16),
        compiler_params=_cparams(("arbitrary", "arbitrary")),
        name="attn",
    )(sink * LOG2E, proj, proj, proj, proj, bias)


GLA_TS = 2048
GLA_C = 128
GLA_NC = GLA_TS // GLA_C


def _gla_tile(q_ref, k_ref, v_ref, gl_ref, w_ref, ba_ref, st_ref, rev):
    C = GLA_C
    x = _dot(gl_ref[0], w_ref[...]) + ba_ref[...]
    la = (jnp.minimum(x, 0.0) - jnp.log(1.0 + jnp.exp(-jnp.abs(x)))) * (1.0 / GLA_NORMALIZER)
    row = lax.broadcasted_iota(I32, (C, C), 0)
    col = lax.broadcasted_iota(I32, (C, C), 1)
    if rev:
        tri = jnp.where(col >= row, 1.0, 0.0).astype(BF16)
        keep = col > row
    else:
        tri = jnp.where(col <= row, 1.0, 0.0).astype(BF16)
        keep = col <= row
    hi, mid, lo3 = _split3(la)
    bcs, halves, ehs = [], [], []
    for c in range(GLA_NC):
        rs = slice(c * C, (c + 1) * C)
        bc = _dot(tri, hi[rs]) + _dot(tri, mid[rs]) + _dot(tri, lo3[rs])
        half = 0.5 * (bc[0:1] if rev else bc[C - 1:C])
        bcs.append(bc)
        halves.append(jnp.broadcast_to(half, bc.shape))
        ehs.append(jnp.exp(half))
    bc = jnp.concatenate(bcs, axis=0)
    half = jnp.concatenate(halves, axis=0)
    eh = jnp.concatenate([jnp.broadcast_to(e, (C, e.shape[1])) for e in ehs], axis=0)
    qe = q_ref[0].astype(F32) * (GLA_K ** -0.5) * jnp.exp(bc - half)
    ke = k_ref[0].astype(F32) * jnp.exp(half - bc)
    qt, kt = qe.astype(BF16), ke.astype(BF16)
    qh = (qe * eh).astype(BF16)
    kb = (ke * eh).astype(BF16)
    lo = lax.broadcasted_iota(I32, (1, LANES), 1) < GLA_K
    zero = jnp.zeros((GLA_TS, LANES), BF16)
    order = range(GLA_NC - 1, -1, -1) if rev else range(GLA_NC)
    outs = [[None] * GLA_HEADS for _ in range(GLA_NC)]
    for m in range(GLA_HEADS // 2):
        sl = slice(m * LANES, (m + 1) * LANES)
        ktm = (jnp.where(lo, kt[:, sl], zero), jnp.where(lo, zero, kt[:, sl]))
        qhm = (jnp.where(lo, qh[:, sl], zero), jnp.where(lo, zero, qh[:, sl]))
        st = st_ref[m]
        for c in order:
            rs = slice(c * C, (c + 1) * C)
            stb = st.astype(BF16)
            upd = []
            for par in range(2):
                h = 2 * m + par
                a = jnp.where(keep, _dot_nt(qt[rs, sl], ktm[par][rs]), 0.0).astype(BF16)
                vh = v_ref[0, rs, h * GLA_V:(h + 1) * GLA_V]
                outs[c][h] = _dot(a, vh) + _dot_nt(qhm[par][rs], stb)
                upd.append(_dot_tn(vh, kb[rs, sl]))
            dec = ehs[c][:, sl] * ehs[c][:, sl]
            st = st * dec + jnp.where(lo, upd[0], upd[1])
        st_ref[m] = st
    return outs


def _gla_fwd_kernel(q_ref, k_ref, v_ref, gl_ref, w_ref, ba_ref, o_ref, st_ref):
    @pl.when(pl.program_id(1) == 0)
    def _():
        st_ref[...] = jnp.zeros_like(st_ref)

    outs = _gla_tile(q_ref, k_ref, v_ref, gl_ref, w_ref, ba_ref, st_ref, rev=False)
    for c in range(GLA_NC):
        for h in range(GLA_HEADS):
            o_ref[0, c * GLA_C:(c + 1) * GLA_C, h * GLA_V:(h + 1) * GLA_V] = outs[c][h].astype(BF16)


def _gla_bwd_kernel(q_ref, k_ref, v_ref, gl_ref, w_ref, ba_ref, of_ref, gr_ref, nw_ref, o_ref, st_ref):
    @pl.when(pl.program_id(1) == 0)
    def _():
        st_ref[...] = jnp.zeros_like(st_ref)

    outs = _gla_tile(q_ref, k_ref, v_ref, gl_ref, w_ref, ba_ref, st_ref, rev=True)
    for h in range(GLA_HEADS):
        hs = slice(h * GLA_V, (h + 1) * GLA_V)
        o = jnp.concatenate([outs[c][h] for c in range(GLA_NC)], axis=0) + of_ref[0, :, hs].astype(F32)
        g = gr_ref[0, :, hs].astype(F32)
        o_ref[0, :, hs] = (_rms(o, nw_ref[...]) * (g * _sigmoid(g))).astype(BF16)


def _gla(proj, wa2_f, ba_f, wa2_b, ba_b, gla_norm):
    B, S, _ = proj.shape
    nt = S // GLA_TS
    assert S % GLA_TS == 0
    kw = GLA_HEADS * GLA_K
    wf = jnp.zeros((LANES, kw), F32).at[0:GLA_RANK].set(wa2_f).astype(BF16)
    wb = jnp.zeros((LANES, kw), F32).at[GLA_RANK:2 * GLA_RANK].set(wa2_b).astype(BF16)

    def specs(tile):
        return [pl.BlockSpec((1, GLA_TS, 256), lambda b, i: (b, tile(i), C_GQ // 256)),
                pl.BlockSpec((1, GLA_TS, 256), lambda b, i: (b, tile(i), C_GK // 256)),
                pl.BlockSpec((1, GLA_TS, 512), lambda b, i: (b, tile(i), C_GV // 512)),
                pl.BlockSpec((1, GLA_TS, LANES), lambda b, i: (b, tile(i), C_GL // LANES)),
                pl.BlockSpec((LANES, kw), lambda b, i: (0, 0)),
                pl.BlockSpec((1, kw), lambda b, i: (0, 0))]

    scratch = [pltpu.VMEM((GLA_HEADS // 2, GLA_V, LANES), F32)]
    fwd_tile = lambda i: i
    o_f = pl.pallas_call(
        _gla_fwd_kernel,
        grid=(B, nt),
        in_specs=specs(fwd_tile),
        out_specs=pl.BlockSpec((1, GLA_TS, 512), lambda b, i: (b, i, 0)),
        out_shape=jax.ShapeDtypeStruct((B, S, 512), BF16),
        scratch_shapes=scratch,
        compiler_params=_cparams(("arbitrary", "arbitrary")),
        name="gla_fwd",
    )(proj, proj, proj, proj, wf, ba_f.reshape(1, kw))
    bwd_tile = lambda i: nt - 1 - i
    return pl.pallas_call(
        _gla_bwd_kernel,
        grid=(B, nt),
        in_specs=specs(bwd_tile) + [
            pl.BlockSpec((1, GLA_TS, 512), lambda b, i: (b, bwd_tile(i), 0)),
            pl.BlockSpec((1, GLA_TS, 512), lambda b, i: (b, bwd_tile(i), C_GR // 512)),
            pl.BlockSpec((1, GLA_V), lambda b, i: (0, 0))],
        out_specs=pl.BlockSpec((1, GLA_TS, 512), lambda b, i: (b, bwd_tile(i), 0)),
        out_shape=jax.ShapeDtypeStruct((B, S, 512), BF16),
        scratch_shapes=scratch,
        compiler_params=_cparams(("arbitrary", "arbitrary")),
        name="gla_bwd",
    )(proj, proj, proj, proj, wb, ba_b.reshape(1, kw), o_f, proj, gla_norm.reshape(1, GLA_V))


POST_SUB = 256

def _post_kernel(x_ref, attn_ref, og_ref, ga_ref, gg_ref, ada_ref, wba_ref, wbg_ref, wo_ref, npost_ref,
                 npre_ref, wr_ref, x1_ref, h2_ref, g3_ref, afft_ref):
    w_post = ada_ref[0, 2:3, :] * npost_ref[...]
    w_pre = (1.0 + ada_ref[0, 4:5, :]) * npre_ref[...]
    shift = ada_ref[0, 3:4, :]
    for r0 in range(0, x_ref.shape[1], POST_SUB):
        rs = slice(r0, r0 + POST_SUB)
        ga = ga_ref[0, rs, :].astype(F32)
        gg = gg_ref[0, rs, :].astype(F32)
        merged = ((jnp.tanh(ga) + 1.0) * _dot(attn_ref[0, rs, :], wba_ref[...])
                  + (jnp.tanh(gg) + 1.0) * _dot(og_ref[0, rs, :], wbg_ref[...]))
        mix = _dot(merged.astype(BF16), wo_ref[...])
        x1 = x_ref[0, rs, :] + _rms(mix, w_post)
        x1_ref[0, rs, :] = x1
        h2 = _rms(x1, w_pre) + shift
        h2_ref[0, rs, :] = h2.astype(BF16)
        h_hi, h_lo = _split2(h2)
        r = _dot(h_hi, wr_ref[...]) + _dot(h_lo, wr_ref[...])
        logits = r[:, 0:LANES] + r[:, LANES:2 * LANES]
        lane = lax.broadcasted_iota(I32, logits.shape, 1)
        logits = jnp.where(lane < N_EXPERTS, logits, NEG)
        e = jnp.exp(logits - jnp.max(logits, axis=-1, keepdims=True))
        aff = e / jnp.sum(e, axis=-1, keepdims=True)
        a_hi, a_mid, a_lo = _split3(aff)
        g3 = (a_hi.astype(F32) + pltpu.roll(a_mid.astype(F32), N_EXPERTS, 1)
              + pltpu.roll(a_lo.astype(F32), 2 * N_EXPERTS, 1))
        g3_ref[0, rs, :] = g3.astype(BF16)
        afft_ref[:, rs] = aff.T[0:N_EXPERTS, :]


def _post(x, attn, o_g, proj, ada, w_ba, w_bg, w_out, n_post, n_pre, wr):
    B, S, _ = x.shape
    tm = 1024
    nt = S // tm
    const = lambda b, i: (0, 0)
    return pl.pallas_call(
        _post_kernel,
        grid=(B, nt),
        in_specs=[pl.BlockSpec((1, tm, D), lambda b, i: (b, i, 0)),
                  pl.BlockSpec((1, tm, 512), lambda b, i: (b, i, 0)),
                  pl.BlockSpec((1, tm, 512), lambda b, i: (b, i, 0)),
                  pl.BlockSpec((1, tm, D), lambda b, i: (b, i, C_GA // D)),
                  pl.BlockSpec((1, tm, D), lambda b, i: (b, i, C_GG // D)),
                  pl.BlockSpec((1, 6, D), lambda b, i: (b, 0, 0)),
                  pl.BlockSpec((512, D), const),
                  pl.BlockSpec((512, D), const),
                  pl.BlockSpec((D, D), const),
                  pl.BlockSpec((1, D), const),
                  pl.BlockSpec((1, D), const),
                  pl.BlockSpec((D, 2 * LANES), const)],
        out_specs=[pl.BlockSpec((1, tm, D), lambda b, i: (b, i, 0)),
                   pl.BlockSpec((1, tm, D), lambda b, i: (b, i, 0)),
                   pl.BlockSpec((1, tm, LANES), lambda b, i: (b, i, 0)),
                   pl.BlockSpec((N_EXPERTS, tm), lambda b, i: (0, b * nt + i))],
        out_shape=[jax.ShapeDtypeStruct((B, S, D), F32),
                   jax.ShapeDtypeStruct((B, S, D), BF16),
                   jax.ShapeDtypeStruct((B, S, LANES), BF16),
                   jax.ShapeDtypeStruct((N_EXPERTS, B * S), F32)],
        compiler_params=_cparams(("arbitrary", "arbitrary")),
        name="post",
    )(x, attn, o_g, proj, proj, ada, w_ba, w_bg, w_out, n_post.reshape(1, D), n_pre.reshape(1, D), wr)


def _select_kernel(aff_ref, selpos_ref, post_ref, bs_ref, *, cap, n_tok):
    nblk = n_tok // LANES

    def count(mask):
        return jnp.sum(jnp.where(mask, 1.0, 0.0), axis=1, keepdims=True).astype(I32)

    def bit_body(it, bits):
        cand = bits | jnp.left_shift(jnp.int32(1), 30 - it)
        return jnp.where(count(aff_ref[...] >= pltpu.bitcast(cand, F32)) >= cap, cand, bits)

    thr = pltpu.bitcast(lax.fori_loop(0, 31, bit_body, jnp.zeros((N_EXPERTS, 1), I32)), F32)
    need = cap - count(aff_ref[...] > thr)
    r = lax.broadcasted_iota(I32, (LANES, LANES), 0)
    c = lax.broadcasted_iota(I32, (LANES, LANES), 1)
    scan_mat = jnp.concatenate([jnp.where(r <= c, 1.0, 0.0), jnp.ones((LANES, LANES), F32)], axis=1).astype(BF16)
    blk_lane = lax.broadcasted_iota(I32, bs_ref.shape, 1)
    filler = jnp.full((LANES - N_EXPERTS, LANES), -1.0, F32)
    reps = bs_ref.shape[1] // LANES

    def taken_before(c_gt, c_eq):
        return jnp.concatenate([c_gt + jnp.minimum(c_eq, need)] * reps, axis=1)

    def blk_body(j, carry):
        c_gt, c_eq, bs = carry
        off = pl.multiple_of(j * LANES, LANES)
        a = aff_ref[:, pl.ds(off, LANES)]
        gt = a > thr
        eq = a == thr
        both = jnp.concatenate([jnp.where(gt, 1.0, 0.0), jnp.where(eq, 1.0, 0.0)], axis=0).astype(BF16)
        scan = _dot(both, scan_mat).astype(I32)
        ex_gt = c_gt + scan[0:N_EXPERTS, 0:LANES] - jnp.where(gt, 1, 0)
        ex_eq = c_eq + scan[N_EXPERTS:, 0:LANES] - jnp.where(eq, 1, 0)
        sel = jnp.logical_or(gt, jnp.logical_and(eq, ex_eq < need))
        pos = ex_gt + jnp.minimum(ex_eq, need)
        sp = jnp.where(sel, pos, -1)
        selpos_ref[:, pl.ds(off, LANES)] = sp
        post_ref[pl.ds(off, LANES), :] = jnp.concatenate([sp.astype(F32), filler], axis=0).T
        bs = jnp.where(blk_lane == j, taken_before(c_gt, c_eq), bs)
        return c_gt + scan[0:N_EXPERTS, LANES:], c_eq + scan[N_EXPERTS:, LANES:], bs

    z = jnp.zeros((N_EXPERTS, LANES), I32)
    c_gt, c_eq, bs = lax.fori_loop(0, nblk, blk_body, (z, z, jnp.zeros(bs_ref.shape, I32)), unroll=4)
    bs_ref[...] = jnp.where(blk_lane >= nblk, taken_before(c_gt, c_eq), bs)


def _select(aff_t, cap):
    n_tok = aff_t.shape[1]
    nbp = ((n_tok // LANES + 1 + LANES - 1) // LANES) * LANES
    return pl.pallas_call(
        functools.partial(_select_kernel, cap=cap, n_tok=n_tok),
        out_shape=[jax.ShapeDtypeStruct((N_EXPERTS, n_tok), I32),
                   jax.ShapeDtypeStruct((n_tok, LANES), F32),
                   jax.ShapeDtypeStruct((N_EXPERTS, nbp), I32)],
        compiler_params=pltpu.CompilerParams(vmem_limit_bytes=VMEM_LIMIT),
        name="select",
    )(aff_t)


def _row_window(bs_ref, e, tile):
    per = ROUTE_TILE // LANES
    start = bs_ref[e, tile * per]
    cnt = bs_ref[e, tile * per + per] - start
    base = (start // ROW_ALIGN) * ROW_ALIGN
    return start, cnt, base, start - base


def _compact_kernel(*refs):
    for sub in range(ROUTE_SUB):
        _compact_tile(sub, *refs)


def _compact_tile(sub, bs_ref, h2_ref, g3_ref, selpos_ref, xs_ref, stage_ref, over_ref, carry_ref, sems, osems,
                  pend_ref):
    t = pl.program_id(0) * ROUTE_SUB + sub
    nt = pl.num_programs(0) * ROUTE_SUB
    par = sub % 2
    cap = xs_ref.shape[1] - CHUNK
    rows = pl.ds(sub * ROUTE_TILE, ROUTE_TILE)
    h2_ref, g3_ref, selpos_ref = h2_ref.at[rows, :], g3_ref.at[rows, :], selpos_ref.at[:, rows]

    def strip_copy(slot, e, base):
        return pltpu.make_async_copy(stage_ref.at[slot, pl.ds(e * CHUNK, CHUNK), :],
                                     xs_ref.at[e, pl.ds(pl.multiple_of(base, ROW_ALIGN), CHUNK), :], sems.at[slot])

    def over_copy(e, base):
        return pltpu.make_async_copy(over_ref.at[e], xs_ref.at[e, pl.ds(pl.multiple_of(base, ROW_ALIGN), CHUNK), :],
                                     osems.at[e])

    @pl.when(t == 0)
    def _():
        over_ref[0] = jnp.zeros((CHUNK, XS_W), BF16)
        carry_ref[...] = jnp.zeros_like(carry_ref)
        for e in range(N_EXPERTS):
            pend_ref[e] = 0
            pltpu.make_async_copy(over_ref.at[0], xs_ref.at[e, pl.ds(cap, CHUNK), :], osems.at[e]).start()
        for e in range(N_EXPERTS):
            pltpu.make_async_copy(over_ref.at[0], xs_ref.at[e, pl.ds(cap, CHUNK), :], osems.at[e]).wait()

    jrow = lax.broadcasted_iota(I32, (CHUNK, ROUTE_TILE), 0)
    pieces = []
    for e in range(N_EXPERTS):
        _, _, base, _ = _row_window(bs_ref, e, t)
        pieces.append(jnp.where(selpos_ref[e:e + 1, :] - base == jrow, 1.0, 0.0).astype(BF16))
    onehot = jnp.concatenate(pieces, axis=0)
    stage_ref[par, :, 0:D] = _dot(onehot, h2_ref[...]).astype(BF16)
    stage_ref[par, :, D:XS_W] = _dot(onehot, g3_ref[...]).astype(BF16)

    @pl.when(t > 0)
    def _():
        for e in range(N_EXPERTS):
            strip_copy(1 - par, e, 0).wait()

    r16 = lax.broadcasted_iota(I32, (ROW_ALIGN, XS_W), 0)
    for e in range(N_EXPERTS):
        start, cnt, base, off = _row_window(bs_ref, e, t)
        head = pl.ds(e * CHUNK, ROW_ALIGN)
        stage_ref[par, head, :] = jnp.where(r16 < off, carry_ref[e], stage_ref[par, head, :])

        @pl.when(pend_ref[e] == 1)
        def _():
            over_copy(e, 0).wait()
            pend_ref[e] = 0

        strip_copy(par, e, base).start()
        end = start + cnt
        nch = (off + cnt + CHUNK - 1) // CHUNK
        gbase = (end // ROW_ALIGN) * ROW_ALIGN
        src = pl.multiple_of(e * CHUNK + jnp.minimum(gbase - base, CHUNK - ROW_ALIGN), ROW_ALIGN)
        carry_ref[e] = stage_ref[par, pl.ds(src, ROW_ALIGN), :]

        def more(cidx, _):
            cb = base + cidx * CHUNK

            @pl.when(pend_ref[e] == 1)
            def _():
                over_copy(e, 0).wait()

            oh = jnp.where(selpos_ref[e:e + 1, :] - cb == jrow, 1.0, 0.0).astype(BF16)
            over_ref[e, :, 0:D] = _dot(oh, h2_ref[...]).astype(BF16)
            over_ref[e, :, D:XS_W] = _dot(oh, g3_ref[...]).astype(BF16)
            over_copy(e, cb).start()
            pend_ref[e] = 1

            @pl.when(jnp.logical_and(cidx == nch - 1, gbase - cb < CHUNK))
            def _():
                carry_ref[e] = over_ref[e, pl.ds(pl.multiple_of(gbase - cb, ROW_ALIGN), ROW_ALIGN), :]
            return 0

        lax.fori_loop(1, nch, more, 0)

    @pl.when(t == nt - 1)
    def _():
        for e in range(N_EXPERTS):
            strip_copy(par, e, 0).wait()

            @pl.when(pend_ref[e] == 1)
            def _():
                over_copy(e, 0).wait()
                pend_ref[e] = 0


def _compact(bs, h2, g3, selpos, cap):
    n_tok = h2.shape[0]
    step = ROUTE_SUB * ROUTE_TILE
    grid_spec = pltpu.PrefetchScalarGridSpec(
        num_scalar_prefetch=1,
        grid=(n_tok // step,),
        in_specs=[pl.BlockSpec((step, D), lambda t, bs: (t, 0)),
                  pl.BlockSpec((step, LANES), lambda t, bs: (t, 0)),
                  pl.BlockSpec((N_EXPERTS, step), lambda t, bs: (0, t))],
        out_specs=pl.BlockSpec(memory_space=pl.ANY),
        scratch_shapes=[pltpu.VMEM((2, N_EXPERTS * CHUNK, XS_W), BF16),
                        pltpu.VMEM((N_EXPERTS, CHUNK, XS_W), BF16),
                        pltpu.VMEM((N_EXPERTS, ROW_ALIGN, XS_W), BF16),
                        pltpu.SemaphoreType.DMA((2,)),
                        pltpu.SemaphoreType.DMA((N_EXPERTS,)),
                        pltpu.SMEM((N_EXPERTS,), I32)])
    return pl.pallas_call(
        _compact_kernel,
        grid_spec=grid_spec,
        out_shape=jax.ShapeDtypeStruct((N_EXPERTS, cap + CHUNK, XS_W), BF16),
        compiler_params=_cparams(("arbitrary",)),
        name="compact",
    )(bs, h2, g3, selpos)


def _ffn_kernel(xs_ref, wg_ref, wu_ref, wd_ref, ys_ref, wgb_ref, wub_ref, wdb_ref):
    g = pl.program_id(0)
    j = pl.program_id(1)
    slab = wg_ref.shape[1]

    @pl.when(g < N_EXPERTS)
    def _():
        rows = pl.ds(pl.multiple_of(j * slab, slab), slab)
        wgb_ref[g % 2, rows, :] = wg_ref[0].astype(BF16)
        wub_ref[g % 2, rows, :] = wu_ref[0].astype(BF16)
        wdb_ref[g % 2, rows, :] = wd_ref[0].astype(BF16)

    @pl.when(g > 0)
    def _():
        e = g - 1
        cur = e % 2
        xs = xs_ref[0]
        x = xs[:, 0:D]
        lane = lax.broadcasted_iota(I32, (1, LANES), 1)
        mine = jnp.logical_or(lane == e, jnp.logical_or(lane == e + N_EXPERTS, lane == e + 2 * N_EXPERTS))
        gate = jnp.sum(jnp.where(mine, xs[:, D:XS_W].astype(F32), 0.0), axis=1, keepdims=True)
        a = _dot(x, wgb_ref[cur])
        hid = (a * _sigmoid(a) * _dot(x, wub_ref[cur])).astype(BF16)
        ys_ref[0] = (_dot(hid, wdb_ref[cur]) * gate).astype(BF16)


def _ffn(xs, w_gate, w_up, w_down, cap):
    tm = min(1024, cap)
    nj = cap // tm
    assert cap % tm == 0 and D % nj == 0 and FF % nj == 0
    last = N_EXPERTS - 1
    wmap = lambda g, j: (jnp.minimum(g, last), jnp.where(g <= last, j, nj - 1), 0)
    xmap = lambda g, j: (jnp.maximum(g - 1, 0), jnp.where(g == 0, 0, j), 0)
    return pl.pallas_call(
        _ffn_kernel,
        grid=(N_EXPERTS + 1, nj),
        in_specs=[pl.BlockSpec((1, tm, XS_W), xmap),
                  pl.BlockSpec((1, D // nj, FF), wmap),
                  pl.BlockSpec((1, D // nj, FF), wmap),
                  pl.BlockSpec((1, FF // nj, D), wmap)],
        out_specs=pl.BlockSpec((1, tm, D), xmap),
        out_shape=jax.ShapeDtypeStruct((N_EXPERTS, cap, D), BF16),
        scratch_shapes=[pltpu.VMEM((2, D, FF), BF16), pltpu.VMEM((2, D, FF), BF16), pltpu.VMEM((2, FF, D), BF16)],
        compiler_params=_cparams(("arbitrary", "arbitrary")),
        name="ffn",
    )(xs, w_gate, w_up, w_down)


def _combine_kernel(*refs, cap):
    for sub in range(ROUTE_SUB):
        _combine_tile(sub, *refs, cap=cap)


def _combine_tile(sub, bs_ref, x1_ref, post_ref, ada_ref, nw_ref, expand_ref, ys_ref, o_ref,
                  strips_ref, over_ref, acc_ref, sems, osems, *, cap):
    nts = pl.num_programs(1)
    t = (pl.program_id(0) * nts + pl.program_id(1)) * ROUTE_SUB + sub
    nt = pl.num_programs(0) * nts * ROUTE_SUB
    rows = pl.ds(sub * ROUTE_TILE, ROUTE_TILE)
    x1_ref, post_ref, o_ref = x1_ref.at[:, rows, :], post_ref.at[rows, :], o_ref.at[:, rows, :]

    def strip_base(e, tile):
        _, _, base, _ = _row_window(bs_ref, e, tile)
        return jnp.minimum(base, cap - CHUNK)

    def strip_copy(e, tile, slot):
        src = ys_ref.at[e, pl.ds(pl.multiple_of(strip_base(e, tile), ROW_ALIGN), CHUNK), :]
        return pltpu.make_async_copy(src, strips_ref.at[slot, pl.ds(e * CHUNK, CHUNK), :], sems.at[slot])

    def over_copy(e, cb):
        return pltpu.make_async_copy(ys_ref.at[e, pl.ds(pl.multiple_of(cb, ROW_ALIGN), CHUNK), :], over_ref.at[e],
                                     osems.at[e])

    def n_strips(e):
        _, cnt, _, off = _row_window(bs_ref, e, t)
        return (off + cnt + CHUNK - 1) // CHUNK

    def over_base(e, cidx):
        _, _, base, _ = _row_window(bs_ref, e, t)
        return jnp.minimum(base + cidx * CHUNK, cap - CHUNK)

    @pl.when(t == 0)
    def _():
        for e in range(N_EXPERTS):
            strip_copy(e, 0, 0).start()

    @pl.when(t + 1 < nt)
    def _():
        for e in range(N_EXPERTS):
            strip_copy(e, t + 1, (sub + 1) % 2).start()

    for e in range(N_EXPERTS):
        @pl.when(n_strips(e) >= 2)
        def _():
            over_copy(e, over_base(e, 1)).start()

    pos = post_ref[...]
    elane = lax.broadcasted_iota(I32, (1, LANES), 1)
    bases = jnp.zeros((1, LANES), I32)
    for e in range(N_EXPERTS):
        bases = jnp.where(elane == e, strip_base(e, t), bases)
    rel = pos - bases.astype(F32)
    inside = jnp.logical_and(pos >= 0.0, jnp.logical_and(rel >= 0.0, rel < float(CHUNK)))
    rel_x = _dot(jnp.where(inside, rel, -1.0).astype(BF16), expand_ref[...])
    want = (lax.broadcasted_iota(I32, (1, N_EXPERTS * CHUNK), 1) % CHUNK).astype(F32)
    onehot = jnp.where(rel_x == want, 1.0, 0.0).astype(BF16)

    slot = sub % 2
    for e in range(N_EXPERTS):
        strip_copy(e, t, slot).wait()
    acc_ref[...] = _dot(onehot, strips_ref[slot])

    lane_c = lax.broadcasted_iota(I32, (1, CHUNK), 1)
    for e in range(N_EXPERTS):
        _, _, base, _ = _row_window(bs_ref, e, t)
        nch = n_strips(e)

        def more(cidx, _):
            cb = over_base(e, cidx)

            @pl.when(cidx >= 2)
            def _():
                over_copy(e, cb).start()

            tgt = (lane_c + cb).astype(F32)
            prev_end = (jnp.minimum(base + (cidx - 1) * CHUNK, cap - CHUNK) + CHUNK).astype(F32)
            col = post_ref[:, e:e + 1]
            oh = jnp.where(jnp.logical_and(col == tgt, col >= prev_end), 1.0, 0.0).astype(BF16)
            over_copy(e, cb).wait()
            acc_ref[...] += _dot(oh, over_ref[e])
            return 0

        lax.fori_loop(1, nch, more, 0)

    y = acc_ref[...]
    o_ref[0] = x1_ref[0] + _rms(y, ada_ref[0, 5:6, :] * nw_ref[...])


def _combine(bs, x1, post, ada, norm_w, expand, ys, cap):
    B, S, _ = x1.shape
    step = ROUTE_SUB * ROUTE_TILE
    nts = S // step
    grid_spec = pltpu.PrefetchScalarGridSpec(
        num_scalar_prefetch=1,
        grid=(B, nts),
        in_specs=[pl.BlockSpec((1, step, D), lambda b, i, bs: (b, i, 0)),
                  pl.BlockSpec((step, LANES), lambda b, i, bs: (b * nts + i, 0)),
                  pl.BlockSpec((1, 6, D), lambda b, i, bs: (b, 0, 0)),
                  pl.BlockSpec((1, D), lambda b, i, bs: (0, 0)),
                  pl.BlockSpec((LANES, N_EXPERTS * CHUNK), lambda b, i, bs: (0, 0)),
                  pl.BlockSpec(memory_space=pl.ANY)],
        out_specs=pl.BlockSpec((1, step, D), lambda b, i, bs: (b, i, 0)),
        scratch_shapes=[pltpu.VMEM((2, N_EXPERTS * CHUNK, D), BF16),
                        pltpu.VMEM((N_EXPERTS, CHUNK, D), BF16),
                        pltpu.VMEM((ROUTE_TILE, D), F32),
                        pltpu.SemaphoreType.DMA((2,)),
                        pltpu.SemaphoreType.DMA((N_EXPERTS,))])
    return pl.pallas_call(
        functools.partial(_combine_kernel, cap=cap),
        grid_spec=grid_spec,
        out_shape=jax.ShapeDtypeStruct((B, S, D), F32),
        compiler_params=_cparams(("arbitrary", "arbitrary")),
        name="combine",
    )(bs, x1, post, ada, norm_w.reshape(1, D), expand, ys)


def _prepare_weights(w_in, w_branch_attn, w_branch_gla, w_out, w_router, w_gate_e, w_up_e, w_down_e):
    aq, ak, av, gq, gk, gv, gr, glf, glb, ga, gg = jnp.split(
        w_in, np.cumsum([512, 128, 128, 256, 256, 512, 512, 16, 16, 1024]).tolist(), axis=1)
    pad = jnp.zeros((D, NP - C_GL - 2 * GLA_RANK), F32)
    w_in_p = jnp.concatenate([aq * LOG2E, gv, 0.5 * ga, 0.5 * gg, gr, ak, av, gq, gk, glf, glb, pad],
                             axis=1).astype(BF16)
    wr = jnp.zeros((D, LANES), F32).at[:, 0:N_EXPERTS].set(w_router)
    wr_hi = wr.astype(BF16)
    wr_cat = jnp.concatenate([wr_hi, (wr - wr_hi.astype(F32)).astype(BF16)], axis=1)
    expand = np.zeros((LANES, N_EXPERTS * CHUNK), np.float32)
    for e in range(N_EXPERTS):
        expand[e, e * CHUNK:(e + 1) * CHUNK] = 1.0
    return dict(w_in_p=w_in_p, w_ba=w_branch_attn.astype(BF16), w_bg=w_branch_gla.astype(BF16),
                w_out=(0.5 * w_out).astype(BF16), wr=wr_cat, w_gate=w_gate_e,
                w_up=w_up_e, w_down=w_down_e, expand=jnp.asarray(expand, BF16),
                bias=_attn_bias_table())


def _layer(x, ada, w, p):
    B, S, _ = x.shape
    n_tok = B * S
    cap = CAPACITY_FACTOR * n_tok // N_EXPERTS
    assert ROUTE_SUB % 2 == 0 and S % (ROUTE_SUB * ROUTE_TILE) == 0 and cap % CHUNK == 0
    proj = _inproj(x, ada, p['norm_pre_mix'], w['w_in_p'])
    attn = _attention(proj, p['attn_sink'], w['bias'])
    o_g = _gla(proj, p['gla_wa2_fwd'], p['gla_ba_fwd'], p['gla_wa2_bwd'], p['gla_ba_bwd'], p['gla_norm'])
    x1, h2, g3, aff_t = _post(x, attn, o_g, proj, ada, w['w_ba'], w['w_bg'], w['w_out'],
                              p['norm_post_mix'], p['norm_pre_ffn'], w['wr'])
    selpos, post, bs = _select(aff_t, cap)
    xs = _compact(bs, h2.reshape(n_tok, D), g3.reshape(n_tok, LANES), selpos, cap)
    ys = _ffn(xs, w['w_gate'], w['w_up'], w['w_down'], cap)
    return _combine(bs, x1, post, ada, p['norm_post_ffn'], w['expand'], ys, cap)


def kernel(x_prompt, x_sample, c_prompt, c_sample, w_ada, b_ada, norm_pre_mix, norm_post_mix, w_in, attn_sink, gla_wa2_fwd, gla_ba_fwd, gla_wa2_bwd, gla_ba_bwd, gla_norm, w_branch_attn, w_branch_gla, w_out, norm_pre_ffn, norm_post_ffn, w_router, w_gate_e, w_up_e, w_down_e):
    assert w_ada.shape[0] == 1, "one layer"
    p = dict(norm_pre_mix=norm_pre_mix[0], norm_post_mix=norm_post_mix[0], attn_sink=attn_sink[0],
             gla_wa2_fwd=gla_wa2_fwd[0], gla_ba_fwd=gla_ba_fwd[0], gla_wa2_bwd=gla_wa2_bwd[0],
             gla_ba_bwd=gla_ba_bwd[0], gla_norm=gla_norm[0], norm_pre_ffn=norm_pre_ffn[0],
             norm_post_ffn=norm_post_ffn[0])
    w = _prepare_weights(w_in[0], w_branch_attn[0], w_branch_gla[0], w_out[0], w_router[0],
                         w_gate_e[0], w_up_e[0], w_down_e[0])
    bp, bs_ = c_prompt.shape[0], c_sample.shape[0]
    rows = -(-(bp + bs_) // 8) * 8
    c_all = jnp.concatenate([c_prompt, c_sample, jnp.zeros((rows - bp - bs_, D), F32)], axis=0)
    ada = _ada(c_all, w_ada[0], b_ada[0]).reshape(rows, 6, D)
    y_prompt = _layer(x_prompt, ada[0:bp], w, p)
    y_sample = _layer(x_sample, ada[bp:bp + bs_], w, p)
    return (y_prompt, y_sample)
```

```python
import functools

import numpy as np
import jax
import jax.numpy as jnp
from jax import lax
from jax.experimental import pallas as pl
from jax.experimental.pallas import tpu as pltpu

F32, BF16, I32 = jnp.float32, jnp.bfloat16, jnp.int32

D = 1024
N_Q_HEADS, N_KV_HEADS, HEAD_DIM = 8, 2, 64
ATTN_BLOCK = 128
GLA_HEADS, GLA_K, GLA_V = 4, 64, 128
GLA_RANK = 16
GLA_NORMALIZER = 16.0
N_EXPERTS, FF = 16, 1024
CAPACITY_FACTOR = 2
EPS = 1e-6
NEG = -1e30
LOG2E = 1.4426950408889634

C_AQ, C_GV, C_GA, C_GG, C_GR, C_KV, C_GQ, C_GK, C_GL = 0, 512, 1024, 2048, 3072, 3584, 3840, 4096, 4352
NP = 4480

LANES = 128
ROW_ALIGN = 16
CHUNK = 64
RCHUNK = 96
ROUTE_TILE = 256
XS_W = D + LANES
VMEM_LIMIT = 56 * 1024 * 1024


def _cparams(sem):
    return pltpu.CompilerParams(dimension_semantics=sem, vmem_limit_bytes=VMEM_LIMIT)


def _dot(a, b):
    return jnp.dot(a, b, preferred_element_type=F32)


def _dot_nt(a, b):
    return lax.dot_general(a, b, (((1,), (1,)), ((), ())), preferred_element_type=F32)


def _dot_tn(a, b):
    return lax.dot_general(a, b, (((0,), (0,)), ((), ())), preferred_element_type=F32)


def _split2(x):
    hi = x.astype(BF16)
    lo = (x - hi.astype(F32)).astype(BF16)
    return hi, lo


def _split3(x):
    hi = x.astype(BF16)
    r = x - hi.astype(F32)
    mid = r.astype(BF16)
    lo = (r - mid.astype(F32)).astype(BF16)
    return hi, mid, lo


def _sigmoid(x):
    return 0.5 * jnp.tanh(0.5 * x) + 0.5


def _rms(x, w):
    return x * lax.rsqrt(jnp.mean(x * x, axis=-1, keepdims=True) + EPS) * w


def _ada_kernel(c_ref, w_ref, b_ref, o_ref):
    c = c_ref[...]
    s = c * jax.nn.sigmoid(c)
    s_hi, s_lo = _split2(s)
    w_hi, w_lo = _split2(w_ref[...])
    o_ref[...] = _dot(s_hi, w_hi) + _dot(s_lo, w_hi) + _dot(s_hi, w_lo) + b_ref[...]


def _ada(c, w_ada, b_ada):
    rows = c.shape[0]
    tn = 1024
    return pl.pallas_call(
        _ada_kernel,
        grid=(6 * D // tn,),
        in_specs=[pl.BlockSpec((rows, D), lambda j: (0, 0)),
                  pl.BlockSpec((D, tn), lambda j: (0, j)),
                  pl.BlockSpec((1, tn), lambda j: (0, j))],
        out_specs=pl.BlockSpec((rows, tn), lambda j: (0, j)),
        out_shape=jax.ShapeDtypeStruct((rows, 6 * D), F32),
        compiler_params=_cparams(("arbitrary",)),
        name="ada",
    )(c, w_ada, b_ada.reshape(1, 6 * D))


def _inproj_kernel(x_ref, ada_ref, nw_ref, w_ref, o_ref):
    x = x_ref[0]
    h = (_rms(x, nw_ref[...] * (1.0 + ada_ref[0, 1:2, :])) + ada_ref[0, 0:1, :]).astype(BF16)
    step = 1024
    for c0 in range(0, NP, step):
        c1 = min(c0 + step, NP)
        o_ref[0, :, c0:c1] = _dot(h, w_ref[:, c0:c1]).astype(BF16)


def _inproj(x, ada, norm_w, w_in_p):
    B, S, _ = x.shape
    tm = 1024
    return pl.pallas_call(
        _inproj_kernel,
        grid=(B, S // tm),
        in_specs=[pl.BlockSpec((1, tm, D), lambda b, i: (b, i, 0)),
                  pl.BlockSpec((1, 6, D), lambda b, i: (b, 0, 0)),
                  pl.BlockSpec((1, D), lambda b, i: (0, 0)),
                  pl.BlockSpec((D, NP), lambda b, i: (0, 0), pipeline_mode=pl.Buffered(1))],
        out_specs=pl.BlockSpec((1, tm, NP), lambda b, i: (b, i, 0)),
        out_shape=jax.ShapeDtypeStruct((B, S, NP), BF16),
        compiler_params=_cparams(("arbitrary", "arbitrary")),
        name="inproj",
    )(x, ada, norm_w.reshape(1, D), w_in_p)


ATTN_TQ = 2048
ATTN_QB = ATTN_TQ // ATTN_BLOCK


_ATTN_STACK = (0, 2, 1, 3)


def _attn_bias_table():
    L = ATTN_BLOCK
    group = N_Q_HEADS // N_KV_HEADS
    q = np.arange(L)[:, None]
    k = np.arange(3 * L)[None, :]
    dist = np.abs(q - k + L).astype(np.float32)
    slopes = np.asarray([2.0 ** (-8.0 * (h + 1) / N_Q_HEADS) for h in range(N_Q_HEADS)], np.float32)
    base = -slopes[:, None, None] * dist[None] * np.float32(LOG2E)
    inside = (dist <= L)[None]
    variants = []
    for ok_k in (np.ones_like(k, bool), k >= L, k < 2 * L):
        per_head = np.where(inside & ok_k[None], base, np.float32(NEG))
        variants.append(np.stack([np.concatenate([per_head[hkv * group + g] for g in _ATTN_STACK], axis=0)
                                  for hkv in range(N_KV_HEADS)]))
    return jnp.asarray(np.stack(variants), F32)


def _attn_kernel(sink_ref, q_ref, kvm_ref, kvp_ref, kvn_ref, bias_ref, o_ref):
    L = ATTN_BLOCK
    i = pl.program_id(1)
    last = pl.num_programs(1) - 1
    kv = jnp.concatenate([kvp_ref[0], kvm_ref[0], kvn_ref[0]], axis=0)
    k = kv[:, :LANES] * jnp.asarray(HEAD_DIM ** -0.5, BF16)
    v = kv[:, LANES:]
    lo = lax.broadcasted_iota(I32, (1, LANES), 1) < HEAD_DIM
    zero = jnp.zeros_like(k)

    def halves(t):
        sw = jnp.concatenate([t[:, HEAD_DIM:], t[:, :HEAD_DIM]], axis=1)
        return ((jnp.where(lo, t, zero), jnp.where(lo, zero, sw)),
                (jnp.where(lo, sw, zero), jnp.where(lo, zero, t)))

    kh, vh = halves(k), halves(v)
    group = N_Q_HEADS // N_KV_HEADS
    rowblk = lax.broadcasted_iota(I32, (group * L, 1), 0) // L
    for j in range(ATTN_QB):
        first = jnp.logical_and(i == 0, j == 0)
        final = jnp.logical_and(i == last, j == ATTN_QB - 1)
        var = jnp.where(first, 1, jnp.where(final, 2, 0))
        r0 = j * L
        for hkv in range(N_KV_HEADS):
            c0 = hkv * 2 * LANES
            qs = jnp.concatenate([q_ref[0, r0:r0 + L, c0:c0 + LANES],
                                  q_ref[0, r0:r0 + L, c0 + LANES:c0 + 2 * LANES]], axis=0)
            s = jnp.concatenate([_dot_nt(qs, kh[hkv][0][r0:r0 + 3 * L]),
                                 _dot_nt(qs, kh[hkv][1][r0:r0 + 3 * L])], axis=0) + bias_ref[var, hkv]
            snk = jnp.zeros((group * L, 1), F32)
            for b, g in enumerate(_ATTN_STACK):
                snk = jnp.where(rowblk == b, sink_ref[hkv * group + g], snk)
            mx = jnp.maximum(jnp.max(s, axis=-1, keepdims=True), snk)
            p = jnp.exp2(s - mx)
            rden = 1.0 / (jnp.sum(p, axis=-1, keepdims=True) + jnp.exp2(snk - mx))
            pb = p.astype(BF16)
            o = (_dot(pb[0:2 * L], vh[hkv][0][r0:r0 + 3 * L]) * rden[0:2 * L]
                 + _dot(pb[2 * L:4 * L], vh[hkv][1][r0:r0 + 3 * L]) * rden[2 * L:4 * L])
            o_ref[0, r0:r0 + L, c0:c0 + LANES] = o[0:L].astype(BF16)
            o_ref[0, r0:r0 + L, c0 + LANES:c0 + 2 * LANES] = o[L:2 * L].astype(BF16)


def _attention(proj, sink, bias):
    B, S, _ = proj.shape
    L = ATTN_BLOCK
    nb = S // L
    assert S % ATTN_TQ == 0 and nb >= 2
    kvc = C_KV // 256
    return pl.pallas_call(
        _attn_kernel,
        grid=(B, S // ATTN_TQ),
        in_specs=[pl.BlockSpec(memory_space=pltpu.SMEM),
                  pl.BlockSpec((1, ATTN_TQ, 512), lambda b, i: (b, i, C_AQ // 512)),
                  pl.BlockSpec((1, ATTN_TQ, 256), lambda b, i: (b, i, kvc)),
                  pl.BlockSpec((1, L, 256), lambda b, i: (b, jnp.maximum(i * ATTN_QB - 1, 0), kvc)),
                  pl.BlockSpec((1, L, 256), lambda b, i: (b, jnp.minimum(i * ATTN_QB + ATTN_QB, nb - 1), kvc)),
                  pl.BlockSpec((3, N_KV_HEADS, (N_Q_HEADS // N_KV_HEADS) * L, 3 * L), lambda b, i: (0, 0, 0, 0))],
        out_specs=pl.BlockSpec((1, ATTN_TQ, 512), lambda b, i: (b, i, 0)),
        out_shape=jax.ShapeDtypeStruct((B, S, 512), BF16),
        compiler_params=_cparams(("arbitrary", "arbitrary")),
        name="attn",
    )(sink * LOG2E, proj, proj, proj, proj, bias)


GLA_TS = 2048
GLA_C = 128
GLA_NC = GLA_TS // GLA_C


def _gla_tile(q_ref, k_ref, v_ref, gl_ref, w_ref, ba_ref, st_ref, rev):
    C = GLA_C
    x = _dot(gl_ref[0], w_ref[...]) + ba_ref[...]
    la = (jnp.minimum(x, 0.0) - jnp.log(1.0 + jnp.exp(-jnp.abs(x)))) * (1.0 / GLA_NORMALIZER)
    row = lax.broadcasted_iota(I32, (C, C), 0)
    col = lax.broadcasted_iota(I32, (C, C), 1)
    if rev:
        tri = jnp.where(col >= row, 1.0, 0.0).astype(BF16)
        keep = col > row
    else:
        tri = jnp.where(col <= row, 1.0, 0.0).astype(BF16)
        keep = col <= row
    hi, mid, lo3 = _split3(la)
    bcs, halves, ehs = [], [], []
    for c in range(GLA_NC):
        rs = slice(c * C, (c + 1) * C)
        bc = _dot(tri, hi[rs]) + _dot(tri, mid[rs]) + _dot(tri, lo3[rs])
        half = 0.5 * (bc[0:1] if rev else bc[C - 1:C])
        bcs.append(bc)
        halves.append(jnp.broadcast_to(half, bc.shape))
        ehs.append(jnp.exp(half))
    bc = jnp.concatenate(bcs, axis=0)
    half = jnp.concatenate(halves, axis=0)
    eh = jnp.concatenate([jnp.broadcast_to(e, (C, e.shape[1])) for e in ehs], axis=0)
    qe = q_ref[0].astype(F32) * (GLA_K ** -0.5) * jnp.exp(bc - half)
    ke = k_ref[0].astype(F32) * jnp.exp(half - bc)
    qt, kt = qe.astype(BF16), ke.astype(BF16)
    qh = (qe * eh).astype(BF16)
    kb = (ke * eh).astype(BF16)
    lo = lax.broadcasted_iota(I32, (1, LANES), 1) < GLA_K
    zero = jnp.zeros((GLA_TS, LANES), BF16)
    order = range(GLA_NC - 1, -1, -1) if rev else range(GLA_NC)
    outs = [[None] * GLA_HEADS for _ in range(GLA_NC)]
    for m in range(GLA_HEADS // 2):
        sl = slice(m * LANES, (m + 1) * LANES)
        ktm = (jnp.where(lo, kt[:, sl], zero), jnp.where(lo, zero, kt[:, sl]))
        qhm = (jnp.where(lo, qh[:, sl], zero), jnp.where(lo, zero, qh[:, sl]))
        st = st_ref[m]
        for c in order:
            rs = slice(c * C, (c + 1) * C)
            stb = st.astype(BF16)
            upd = []
            for par in range(2):
                h = 2 * m + par
                a = jnp.where(keep, _dot_nt(qt[rs, sl], ktm[par][rs]), 0.0).astype(BF16)
                vh = v_ref[0, rs, h * GLA_V:(h + 1) * GLA_V]
                outs[c][h] = _dot(a, vh) + _dot_nt(qhm[par][rs], stb)
                upd.append(_dot_tn(vh, kb[rs, sl]))
            dec = ehs[c][:, sl] * ehs[c][:, sl]
            st = st * dec + jnp.where(lo, upd[0], upd[1])
        st_ref[m] = st
    return outs


def _gla_fwd_kernel(q_ref, k_ref, v_ref, gl_ref, w_ref, ba_ref, o_ref, st_ref):
    @pl.when(pl.program_id(1) == 0)
    def _():
        st_ref[...] = jnp.zeros_like(st_ref)

    outs = _gla_tile(q_ref, k_ref, v_ref, gl_ref, w_ref, ba_ref, st_ref, rev=False)
    for c in range(GLA_NC):
        for h in range(GLA_HEADS):
            o_ref[0, c * GLA_C:(c + 1) * GLA_C, h * GLA_V:(h + 1) * GLA_V] = outs[c][h].astype(BF16)


def _gla_bwd_kernel(q_ref, k_ref, v_ref, gl_ref, w_ref, ba_ref, of_ref, gr_ref, nw_ref, o_ref, st_ref):
    @pl.when(pl.program_id(1) == 0)
    def _():
        st_ref[...] = jnp.zeros_like(st_ref)

    outs = _gla_tile(q_ref, k_ref, v_ref, gl_ref, w_ref, ba_ref, st_ref, rev=True)
    for h in range(GLA_HEADS):
        hs = slice(h * GLA_V, (h + 1) * GLA_V)
        o = jnp.concatenate([outs[c][h] for c in range(GLA_NC)], axis=0) + of_ref[0, :, hs].astype(F32)
        g = gr_ref[0, :, hs].astype(F32)
        o_ref[0, :, hs] = (_rms(o, nw_ref[...]) * (g * _sigmoid(g))).astype(BF16)


def _gla(proj, wa2_f, ba_f, wa2_b, ba_b, gla_norm):
    B, S, _ = proj.shape
    nt = S // GLA_TS
    assert S % GLA_TS == 0
    kw = GLA_HEADS * GLA_K
    wf = jnp.zeros((LANES, kw), F32).at[0:GLA_RANK].set(wa2_f).astype(BF16)
    wb = jnp.zeros((LANES, kw), F32).at[GLA_RANK:2 * GLA_RANK].set(wa2_b).astype(BF16)

    def specs(tile):
        return [pl.BlockSpec((1, GLA_TS, 256), lambda b, i: (b, tile(i), C_GQ // 256)),
                pl.BlockSpec((1, GLA_TS, 256), lambda b, i: (b, tile(i), C_GK // 256)),
                pl.BlockSpec((1, GLA_TS, 512), lambda b, i: (b, tile(i), C_GV // 512)),
                pl.BlockSpec((1, GLA_TS, LANES), lambda b, i: (b, tile(i), C_GL // LANES)),
                pl.BlockSpec((LANES, kw), lambda b, i: (0, 0)),
                pl.BlockSpec((1, kw), lambda b, i: (0, 0))]

    scratch = [pltpu.VMEM((GLA_HEADS // 2, GLA_V, LANES), F32)]
    fwd_tile = lambda i: i
    o_f = pl.pallas_call(
        _gla_fwd_kernel,
        grid=(B, nt),
        in_specs=specs(fwd_tile),
        out_specs=pl.BlockSpec((1, GLA_TS, 512), lambda b, i: (b, i, 0)),
        out_shape=jax.ShapeDtypeStruct((B, S, 512), BF16),
        scratch_shapes=scratch,
        compiler_params=_cparams(("arbitrary", "arbitrary")),
        name="gla_fwd",
    )(proj, proj, proj, proj, wf, ba_f.reshape(1, kw))
    bwd_tile = lambda i: nt - 1 - i
    return pl.pallas_call(
        _gla_bwd_kernel,
        grid=(B, nt),
        in_specs=specs(bwd_tile) + [
            pl.BlockSpec((1, GLA_TS, 512), lambda b, i: (b, bwd_tile(i), 0)),
            pl.BlockSpec((1, GLA_TS, 512), lambda b, i: (b, bwd_tile(i), C_GR // 512)),
            pl.BlockSpec((1, GLA_V), lambda b, i: (0, 0))],
        out_specs=pl.BlockSpec((1, GLA_TS, 512), lambda b, i: (b, bwd_tile(i), 0)),
        out_shape=jax.ShapeDtypeStruct((B, S, 512), BF16),
        scratch_shapes=scratch,
        compiler_params=_cparams(("arbitrary", "arbitrary")),
        name="gla_bwd",
    )(proj, proj, proj, proj, wb, ba_b.reshape(1, kw), o_f, proj, gla_norm.reshape(1, GLA_V))


POST_SUB = 256

def _post_kernel(x_ref, attn_ref, og_ref, ga_ref, gg_ref, ada_ref, wba_ref, wbg_ref, wo_ref, npost_ref,
                 npre_ref, wr_ref, x1_ref, h2_ref, g3_ref, afft_ref):
    w_post = ada_ref[0, 2:3, :] * npost_ref[...]
    w_pre = (1.0 + ada_ref[0, 4:5, :]) * npre_ref[...]
    shift = ada_ref[0, 3:4, :]
    for r0 in range(0, x_ref.shape[1], POST_SUB):
        rs = slice(r0, r0 + POST_SUB)
        ga = ga_ref[0, rs, :].astype(F32)
        gg = gg_ref[0, rs, :].astype(F32)
        merged = ((jnp.tanh(ga) + 1.0) * _dot(attn_ref[0, rs, :], wba_ref[...])
                  + (jnp.tanh(gg) + 1.0) * _dot(og_ref[0, rs, :], wbg_ref[...]))
        mix = _dot(merged.astype(BF16), wo_ref[...])
        x1 = x_ref[0, rs, :] + _rms(mix, w_post)
        x1_ref[0, rs, :] = x1
        h2 = _rms(x1, w_pre) + shift
        h2_ref[0, rs, :] = h2.astype(BF16)
        h_hi, h_lo = _split2(h2)
        r = _dot(h_hi, wr_ref[...]) + _dot(h_lo, wr_ref[...])
        logits = r[:, 0:LANES] + r[:, LANES:2 * LANES]
        lane = lax.broadcasted_iota(I32, logits.shape, 1)
        logits = jnp.where(lane < N_EXPERTS, logits, NEG)
        e = jnp.exp(logits - jnp.max(logits, axis=-1, keepdims=True))
        aff = e / jnp.sum(e, axis=-1, keepdims=True)
        a_hi, a_mid, a_lo = _split3(aff)
        g3 = (a_hi.astype(F32) + pltpu.roll(a_mid.astype(F32), N_EXPERTS, 1)
              + pltpu.roll(a_lo.astype(F32), 2 * N_EXPERTS, 1))
        g3_ref[0, rs, :] = g3.astype(BF16)
        afft_ref[:, rs] = aff.T[0:N_EXPERTS, :]


def _post(x, attn, o_g, proj, ada, w_ba, w_bg, w_out, n_post, n_pre, wr):
    B, S, _ = x.shape
    tm = 1024
    nt = S // tm
    const = lambda b, i: (0, 0)
    return pl.pallas_call(
        _post_kernel,
        grid=(B, nt),
        in_specs=[pl.BlockSpec((1, tm, D), lambda b, i: (b, i, 0)),
                  pl.BlockSpec((1, tm, 512), lambda b, i: (b, i, 0)),
                  pl.BlockSpec((1, tm, 512), lambda b, i: (b, i, 0)),
                  pl.BlockSpec((1, tm, D), lambda b, i: (b, i, C_GA // D)),
                  pl.BlockSpec((1, tm, D), lambda b, i: (b, i, C_GG // D)),
                  pl.BlockSpec((1, 6, D), lambda b, i: (b, 0, 0)),
                  pl.BlockSpec((512, D), const),
                  pl.BlockSpec((512, D), const),
                  pl.BlockSpec((D, D), const),
                  pl.BlockSpec((1, D), const),
                  pl.BlockSpec((1, D), const),
                  pl.BlockSpec((D, 2 * LANES), const)],
        out_specs=[pl.BlockSpec((1, tm, D), lambda b, i: (b, i, 0)),
                   pl.BlockSpec((1, tm, D), lambda b, i: (b, i, 0)),
                   pl.BlockSpec((1, tm, LANES), lambda b, i: (b, i, 0)),
                   pl.BlockSpec((N_EXPERTS, tm), lambda b, i: (0, b * nt + i))],
        out_shape=[jax.ShapeDtypeStruct((B, S, D), F32),
                   jax.ShapeDtypeStruct((B, S, D), BF16),
                   jax.ShapeDtypeStruct((B, S, LANES), BF16),
                   jax.ShapeDtypeStruct((N_EXPERTS, B * S), F32)],
        compiler_params=_cparams(("arbitrary", "arbitrary")),
        name="post",
    )(x, attn, o_g, proj, proj, ada, w_ba, w_bg, w_out, n_post.reshape(1, D), n_pre.reshape(1, D), wr)


def _select_kernel(aff_ref, selpos_ref, post_ref, bs_ref, *, cap, n_tok):
    nblk = n_tok // LANES

    def count(mask):
        return jnp.sum(jnp.where(mask, 1.0, 0.0), axis=1, keepdims=True).astype(I32)

    def bit_body(it, bits):
        cand = bits | jnp.left_shift(jnp.int32(1), 30 - it)
        return jnp.where(count(aff_ref[...] >= pltpu.bitcast(cand, F32)) >= cap, cand, bits)

    thr = pltpu.bitcast(lax.fori_loop(0, 31, bit_body, jnp.zeros((N_EXPERTS, 1), I32)), F32)
    need = cap - count(aff_ref[...] > thr)
    r = lax.broadcasted_iota(I32, (LANES, LANES), 0)
    c = lax.broadcasted_iota(I32, (LANES, LANES), 1)
    scan_mat = jnp.concatenate([jnp.where(r <= c, 1.0, 0.0), jnp.ones((LANES, LANES), F32)], axis=1).astype(BF16)
    blk_lane = lax.broadcasted_iota(I32, bs_ref.shape, 1)
    filler = jnp.full((LANES - N_EXPERTS, LANES), -1.0, F32)
    reps = bs_ref.shape[1] // LANES

    def taken_before(c_gt, c_eq):
        return jnp.concatenate([c_gt + jnp.minimum(c_eq, need)] * reps, axis=1)

    def blk_body(j, carry):
        c_gt, c_eq, bs = carry
        off = pl.multiple_of(j * LANES, LANES)
        a = aff_ref[:, pl.ds(off, LANES)]
        gt = a > thr
        eq = a == thr
        both = jnp.concatenate([jnp.where(gt, 1.0, 0.0), jnp.where(eq, 1.0, 0.0)], axis=0).astype(BF16)
        scan = _dot(both, scan_mat).astype(I32)
        ex_gt = c_gt + scan[0:N_EXPERTS, 0:LANES] - jnp.where(gt, 1, 0)
        ex_eq = c_eq + scan[N_EXPERTS:, 0:LANES] - jnp.where(eq, 1, 0)
        sel = jnp.logical_or(gt, jnp.logical_and(eq, ex_eq < need))
        pos = ex_gt + jnp.minimum(ex_eq, need)
        sp = jnp.where(sel, pos, -1)
        selpos_ref[:, pl.ds(off, LANES)] = sp
        post_ref[pl.ds(off, LANES), :] = jnp.concatenate([sp.astype(F32), filler], axis=0).T
        bs = jnp.where(blk_lane == j, taken_before(c_gt, c_eq), bs)
        return c_gt + scan[0:N_EXPERTS, LANES:], c_eq + scan[N_EXPERTS:, LANES:], bs

    z = jnp.zeros((N_EXPERTS, LANES), I32)
    c_gt, c_eq, bs = lax.fori_loop(0, nblk, blk_body, (z, z, jnp.zeros(bs_ref.shape, I32)), unroll=4)
    bs_ref[...] = jnp.where(blk_lane >= nblk, taken_before(c_gt, c_eq), bs)


def _select(aff_t, cap):
    n_tok = aff_t.shape[1]
    nbp = ((n_tok // LANES + 1 + LANES - 1) // LANES) * LANES
    return pl.pallas_call(
        functools.partial(_select_kernel, cap=cap, n_tok=n_tok),
        out_shape=[jax.ShapeDtypeStruct((N_EXPERTS, n_tok), I32),
                   jax.ShapeDtypeStruct((n_tok, LANES), F32),
                   jax.ShapeDtypeStruct((N_EXPERTS, nbp), I32)],
        compiler_params=pltpu.CompilerParams(vmem_limit_bytes=VMEM_LIMIT),
        name="select",
    )(aff_t)


def _row_window(bs_ref, e, tile):
    per = ROUTE_TILE // LANES
    start = bs_ref[e, tile * per]
    cnt = bs_ref[e, tile * per + per] - start
    base = (start // ROW_ALIGN) * ROW_ALIGN
    return start, cnt, base, start - base


def _compact_kernel(bs_ref, h2_ref, g3_ref, selpos_ref, xs_ref, stage_ref, over_ref, carry_ref, sems, osems, pend_ref):
    t = pl.program_id(0)
    nt = pl.num_programs(0)
    par = t % 2
    cap = xs_ref.shape[1] - CHUNK

    def strip_copy(slot, e, base):
        return pltpu.make_async_copy(stage_ref.at[slot, pl.ds(e * CHUNK, CHUNK), :],
                                     xs_ref.at[e, pl.ds(pl.multiple_of(base, ROW_ALIGN), CHUNK), :], sems.at[slot])

    def over_copy(e, base):
        return pltpu.make_async_copy(over_ref.at[e], xs_ref.at[e, pl.ds(pl.multiple_of(base, ROW_ALIGN), CHUNK), :],
                                     osems.at[e])

    @pl.when(t == 0)
    def _():
        over_ref[0] = jnp.zeros((CHUNK, XS_W), BF16)
        carry_ref[...] = jnp.zeros_like(carry_ref)
        for e in range(N_EXPERTS):
            pend_ref[e] = 0
            pltpu.make_async_copy(over_ref.at[0], xs_ref.at[e, pl.ds(cap, CHUNK), :], osems.at[e]).start()
        for e in range(N_EXPERTS):
            pltpu.make_async_copy(over_ref.at[0], xs_ref.at[e, pl.ds(cap, CHUNK), :], osems.at[e]).wait()

    jrow = lax.broadcasted_iota(I32, (CHUNK, ROUTE_TILE), 0)
    pieces = []
    for e in range(N_EXPERTS):
        _, _, base, _ = _row_window(bs_ref, e, t)
        pieces.append(jnp.where(selpos_ref[e:e + 1, :] - base == jrow, 1.0, 0.0).astype(BF16))
    onehot = jnp.concatenate(pieces, axis=0)
    stage_ref[par, :, 0:D] = _dot(onehot, h2_ref[...]).astype(BF16)
    stage_ref[par, :, D:XS_W] = _dot(onehot, g3_ref[...]).astype(BF16)

    @pl.when(t > 0)
    def _():
        for e in range(N_EXPERTS):
            strip_copy(1 - par, e, 0).wait()

    r16 = lax.broadcasted_iota(I32, (ROW_ALIGN, XS_W), 0)
    for e in range(N_EXPERTS):
        start, cnt, base, off = _row_window(bs_ref, e, t)
        head = pl.ds(e * CHUNK, ROW_ALIGN)
        stage_ref[par, head, :] = jnp.where(r16 < off, carry_ref[e], stage_ref[par, head, :])

        @pl.when(pend_ref[e] == 1)
        def _():
            over_copy(e, 0).wait()
            pend_ref[e] = 0

        strip_copy(par, e, base).start()
        end = start + cnt
        nch = (off + cnt + CHUNK - 1) // CHUNK
        gbase = (end // ROW_ALIGN) * ROW_ALIGN
        src = pl.multiple_of(e * CHUNK + jnp.minimum(gbase - base, CHUNK - ROW_ALIGN), ROW_ALIGN)
        carry_ref[e] = stage_ref[par, pl.ds(src, ROW_ALIGN), :]

        def more(cidx, _):
            cb = base + cidx * CHUNK

            @pl.when(pend_ref[e] == 1)
            def _():
                over_copy(e, 0).wait()

            oh = jnp.where(selpos_ref[e:e + 1, :] - cb == jrow, 1.0, 0.0).astype(BF16)
            over_ref[e, :, 0:D] = _dot(oh, h2_ref[...]).astype(BF16)
            over_ref[e, :, D:XS_W] = _dot(oh, g3_ref[...]).astype(BF16)
            over_copy(e, cb).start()
            pend_ref[e] = 1

            @pl.when(jnp.logical_and(cidx == nch - 1, gbase - cb < CHUNK))
            def _():
                carry_ref[e] = over_ref[e, pl.ds(pl.multiple_of(gbase - cb, ROW_ALIGN), ROW_ALIGN), :]
            return 0

        lax.fori_loop(1, nch, more, 0)

    @pl.when(t == nt - 1)
    def _():
        for e in range(N_EXPERTS):
            strip_copy(par, e, 0).wait()

            @pl.when(pend_ref[e] == 1)
            def _():
                over_copy(e, 0).wait()
                pend_ref[e] = 0


def _compact(bs, h2, g3, selpos, cap):
    n_tok = h2.shape[0]
    nt = n_tok // ROUTE_TILE
    grid_spec = pltpu.PrefetchScalarGridSpec(
        num_scalar_prefetch=1,
        grid=(nt,),
        in_specs=[pl.BlockSpec((ROUTE_TILE, D), lambda t, bs: (t, 0)),
                  pl.BlockSpec((ROUTE_TILE, LANES), lambda t, bs: (t, 0)),
                  pl.BlockSpec((N_EXPERTS, ROUTE_TILE), lambda t, bs: (0, t))],
        out_specs=pl.BlockSpec(memory_space=pl.ANY),
        scratch_shapes=[pltpu.VMEM((2, N_EXPERTS * CHUNK, XS_W), BF16),
                        pltpu.VMEM((N_EXPERTS, CHUNK, XS_W), BF16),
                        pltpu.VMEM((N_EXPERTS, ROW_ALIGN, XS_W), BF16),
                        pltpu.SemaphoreType.DMA((2,)),
                        pltpu.SemaphoreType.DMA((N_EXPERTS,)),
                        pltpu.SMEM((N_EXPERTS,), I32)])
    return pl.pallas_call(
        _compact_kernel,
        grid_spec=grid_spec,
        out_shape=jax.ShapeDtypeStruct((N_EXPERTS, cap + CHUNK, XS_W), BF16),
        compiler_params=_cparams(("arbitrary",)),
        name="compact",
    )(bs, h2, g3, selpos)


def _ffn_kernel(xs_ref, wg_ref, wu_ref, wd_ref, ys_ref, wgb_ref, wub_ref, wdb_ref):
    g = pl.program_id(0)
    j = pl.program_id(1)
    slab = wg_ref.shape[1]

    @pl.when(g < N_EXPERTS)
    def _():
        rows = pl.ds(pl.multiple_of(j * slab, slab), slab)
        wgb_ref[g % 2, rows, :] = wg_ref[0].astype(BF16)
        wub_ref[g % 2, rows, :] = wu_ref[0].astype(BF16)
        wdb_ref[g % 2, rows, :] = wd_ref[0].astype(BF16)

    @pl.when(g > 0)
    def _():
        e = g - 1
        cur = e % 2
        xs = xs_ref[0]
        x = xs[:, 0:D]
        lane = lax.broadcasted_iota(I32, (1, LANES), 1)
        mine = jnp.logical_or(lane == e, jnp.logical_or(lane == e + N_EXPERTS, lane == e + 2 * N_EXPERTS))
        gate = jnp.sum(jnp.where(mine, xs[:, D:XS_W].astype(F32), 0.0), axis=1, keepdims=True)
        a = _dot(x, wgb_ref[cur])
        hid = (a * _sigmoid(a) * _dot(x, wub_ref[cur])).astype(BF16)
        ys_ref[0] = (_dot(hid, wdb_ref[cur]) * gate).astype(BF16)


def _ffn(xs, w_gate, w_up, w_down, cap):
    tm = min(1024, cap)
    nj = cap // tm
    assert cap % tm == 0 and D % nj == 0 and FF % nj == 0
    last = N_EXPERTS - 1
    wmap = lambda g, j: (jnp.minimum(g, last), jnp.where(g <= last, j, nj - 1), 0)
    xmap = lambda g, j: (jnp.maximum(g - 1, 0), jnp.where(g == 0, 0, j), 0)
    return pl.pallas_call(
        _ffn_kernel,
        grid=(N_EXPERTS + 1, nj),
        in_specs=[pl.BlockSpec((1, tm, XS_W), xmap),
                  pl.BlockSpec((1, D // nj, FF), wmap),
                  pl.BlockSpec((1, D // nj, FF), wmap),
                  pl.BlockSpec((1, FF // nj, D), wmap)],
        out_specs=pl.BlockSpec((1, tm, D), xmap),
        out_shape=jax.ShapeDtypeStruct((N_EXPERTS, cap, D), BF16),
        scratch_shapes=[pltpu.VMEM((2, D, FF), BF16), pltpu.VMEM((2, D, FF), BF16), pltpu.VMEM((2, FF, D), BF16)],
        compiler_params=_cparams(("arbitrary", "arbitrary")),
        name="ffn",
    )(xs, w_gate, w_up, w_down)


def _combine_kernel(bs_ref, x1_ref, post_ref, ada_ref, nw_ref, expand_ref, ys_ref, o_ref,
                    strips_ref, over_ref, acc_ref, sems, osems, *, cap):
    nts = pl.num_programs(1)
    t = pl.program_id(0) * nts + pl.program_id(1)
    nt = pl.num_programs(0) * nts

    def strip_base(e, tile):
        _, _, base, _ = _row_window(bs_ref, e, tile)
        return jnp.minimum(base, cap - RCHUNK)

    def strip_copy(e, tile, slot):
        src = ys_ref.at[e, pl.ds(pl.multiple_of(strip_base(e, tile), ROW_ALIGN), RCHUNK), :]
        return pltpu.make_async_copy(src, strips_ref.at[slot, pl.ds(e * RCHUNK, RCHUNK), :], sems.at[slot])

    def over_copy(e, cb):
        return pltpu.make_async_copy(ys_ref.at[e, pl.ds(pl.multiple_of(cb, ROW_ALIGN), RCHUNK), :], over_ref.at[e],
                                     osems.at[e])

    def n_strips(e):
        _, cnt, _, off = _row_window(bs_ref, e, t)
        return (off + cnt + RCHUNK - 1) // RCHUNK

    def over_base(e, cidx):
        _, _, base, _ = _row_window(bs_ref, e, t)
        return jnp.minimum(base + cidx * RCHUNK, cap - RCHUNK)

    @pl.when(t == 0)
    def _():
        for e in range(N_EXPERTS):
            strip_copy(e, 0, 0).start()

    @pl.when(t + 1 < nt)
    def _():
        for e in range(N_EXPERTS):
            strip_copy(e, t + 1, (t + 1) % 2).start()

    for e in range(N_EXPERTS):
        @pl.when(n_strips(e) >= 2)
        def _():
            over_copy(e, over_base(e, 1)).start()

    pos = post_ref[...]
    elane = lax.broadcasted_iota(I32, (1, LANES), 1)
    bases = jnp.zeros((1, LANES), I32)
    for e in range(N_EXPERTS):
        bases = jnp.where(elane == e, strip_base(e, t), bases)
    rel = pos - bases.astype(F32)
    inside = jnp.logical_and(pos >= 0.0, jnp.logical_and(rel >= 0.0, rel < float(RCHUNK)))
    rel_x = _dot(jnp.where(inside, rel, -1.0).astype(BF16), expand_ref[...])
    want = (lax.broadcasted_iota(I32, (1, N_EXPERTS * RCHUNK), 1) % RCHUNK).astype(F32)
    onehot = jnp.where(rel_x == want, 1.0, 0.0).astype(BF16)

    slot = t % 2
    for e in range(N_EXPERTS):
        strip_copy(e, t, slot).wait()
    acc_ref[...] = _dot(onehot, strips_ref[slot])

    lane_c = lax.broadcasted_iota(I32, (1, RCHUNK), 1)
    for e in range(N_EXPERTS):
        _, _, base, _ = _row_window(bs_ref, e, t)
        nch = n_strips(e)

        def more(cidx, _):
            cb = over_base(e, cidx)

            @pl.when(cidx >= 2)
            def _():
                over_copy(e, cb).start()

            tgt = (lane_c + cb).astype(F32)
            prev_end = (jnp.minimum(base + (cidx - 1) * RCHUNK, cap - RCHUNK) + RCHUNK).astype(F32)
            col = post_ref[:, e:e + 1]
            oh = jnp.where(jnp.logical_and(col == tgt, col >= prev_end), 1.0, 0.0).astype(BF16)
            over_copy(e, cb).wait()
            acc_ref[...] += _dot(oh, over_ref[e])
            return 0

        lax.fori_loop(1, nch, more, 0)

    y = acc_ref[...]
    o_ref[0] = x1_ref[0] + _rms(y, ada_ref[0, 5:6, :] * nw_ref[...])


def _combine(bs, x1, post, ada, norm_w, expand, ys, cap):
    B, S, _ = x1.shape
    nts = S // ROUTE_TILE
    grid_spec = pltpu.PrefetchScalarGridSpec(
        num_scalar_prefetch=1,
        grid=(B, nts),
        in_specs=[pl.BlockSpec((1, ROUTE_TILE, D), lambda b, i, bs: (b, i, 0)),
                  pl.BlockSpec((ROUTE_TILE, LANES), lambda b, i, bs: (b * nts + i, 0)),
                  pl.BlockSpec((1, 6, D), lambda b, i, bs: (b, 0, 0)),
                  pl.BlockSpec((1, D), lambda b, i, bs: (0, 0)),
                  pl.BlockSpec((LANES, N_EXPERTS * RCHUNK), lambda b, i, bs: (0, 0)),
                  pl.BlockSpec(memory_space=pl.ANY)],
        out_specs=pl.BlockSpec((1, ROUTE_TILE, D), lambda b, i, bs: (b, i, 0)),
        scratch_shapes=[pltpu.VMEM((2, N_EXPERTS * RCHUNK, D), BF16),
                        pltpu.VMEM((N_EXPERTS, RCHUNK, D), BF16),
                        pltpu.VMEM((ROUTE_TILE, D), F32),
                        pltpu.SemaphoreType.DMA((2,)),
                        pltpu.SemaphoreType.DMA((N_EXPERTS,))])
    return pl.pallas_call(
        functools.partial(_combine_kernel, cap=cap),
        grid_spec=grid_spec,
        out_shape=jax.ShapeDtypeStruct((B, S, D), F32),
        compiler_params=_cparams(("arbitrary", "arbitrary")),
        name="combine",
    )(bs, x1, post, ada, norm_w.reshape(1, D), expand, ys)


def _prepare_weights(w_in, w_branch_attn, w_branch_gla, w_out, w_router, w_gate_e, w_up_e, w_down_e):
    aq, ak, av, gq, gk, gv, gr, glf, glb, ga, gg = jnp.split(
        w_in, np.cumsum([512, 128, 128, 256, 256, 512, 512, 16, 16, 1024]).tolist(), axis=1)
    pad = jnp.zeros((D, NP - C_GL - 2 * GLA_RANK), F32)
    w_in_p = jnp.concatenate([aq * LOG2E, gv, 0.5 * ga, 0.5 * gg, gr, ak, av, gq, gk, glf, glb, pad],
                             axis=1).astype(BF16)
    wr = jnp.zeros((D, LANES), F32).at[:, 0:N_EXPERTS].set(w_router)
    wr_hi = wr.astype(BF16)
    wr_cat = jnp.concatenate([wr_hi, (wr - wr_hi.astype(F32)).astype(BF16)], axis=1)
    expand = np.zeros((LANES, N_EXPERTS * RCHUNK), np.float32)
    for e in range(N_EXPERTS):
        expand[e, e * RCHUNK:(e + 1) * RCHUNK] = 1.0
    return dict(w_in_p=w_in_p, w_ba=w_branch_attn.astype(BF16), w_bg=w_branch_gla.astype(BF16),
                w_out=(0.5 * w_out).astype(BF16), wr=wr_cat, w_gate=w_gate_e,
                w_up=w_up_e, w_down=w_down_e, expand=jnp.asarray(expand, BF16),
                bias=_attn_bias_table())


def _layer(x, ada, w, p):
    B, S, _ = x.shape
    n_tok = B * S
    cap = CAPACITY_FACTOR * n_tok // N_EXPERTS
    assert n_tok % ROUTE_TILE == 0 and S % ROUTE_TILE == 0 and cap % ROW_ALIGN == 0 and cap >= RCHUNK
    proj = _inproj(x, ada, p['norm_pre_mix'], w['w_in_p'])
    attn = _attention(proj, p['attn_sink'], w['bias'])
    o_g = _gla(proj, p['gla_wa2_fwd'], p['gla_ba_fwd'], p['gla_wa2_bwd'], p['gla_ba_bwd'], p['gla_norm'])
    x1, h2, g3, aff_t = _post(x, attn, o_g, proj, ada, w['w_ba'], w['w_bg'], w['w_out'],
                              p['norm_post_mix'], p['norm_pre_ffn'], w['wr'])
    selpos, post, bs = _select(aff_t, cap)
    xs = _compact(bs, h2.reshape(n_tok, D), g3.reshape(n_tok, LANES), selpos, cap)
    ys = _ffn(xs, w['w_gate'], w['w_up'], w['w_down'], cap)
    return _combine(bs, x1, post, ada, p['norm_post_ffn'], w['expand'], ys, cap)


def kernel(x_prompt, x_sample, c_prompt, c_sample, w_ada, b_ada, norm_pre_mix, norm_post_mix, w_in, attn_sink, gla_wa2_fwd, gla_ba_fwd, gla_wa2_bwd, gla_ba_bwd, gla_norm, w_branch_attn, w_branch_gla, w_out, norm_pre_ffn, norm_post_ffn, w_router, w_gate_e, w_up_e, w_down_e):
    assert w_ada.shape[0] == 1, "one layer"
    p = dict(norm_pre_mix=norm_pre_mix[0], norm_post_mix=norm_post_mix[0], attn_sink=attn_sink[0],
             gla_wa2_fwd=gla_wa2_fwd[0], gla_ba_fwd=gla_ba_fwd[0], gla_wa2_bwd=gla_wa2_bwd[0],
             gla_ba_bwd=gla_ba_bwd[0], gla_norm=gla_norm[0], norm_pre_ffn=norm_pre_ffn[0],
             norm_post_ffn=norm_post_ffn[0])
    w = _prepare_weights(w_in[0], w_branch_attn[0], w_branch_gla[0], w_out[0], w_router[0],
                         w_gate_e[0], w_up_e[0], w_down_e[0])
    bp, bs_ = c_prompt.shape[0], c_sample.shape[0]
    rows = -(-(bp + bs_) // 8) * 8
    c_all = jnp.concatenate([c_prompt, c_sample, jnp.zeros((rows - bp - bs_, D), F32)], axis=0)
    ada = _ada(c_all, w_ada[0], b_ada[0]).reshape(rows, 6, D)
    y_prompt = _layer(x_prompt, ada[0:bp], w, p)
    y_sample = _layer(x_sample, ada[bp:bp + bs_], w, p)
    return (y_prompt, y_sample)
```

```python
import functools

import numpy as np
import jax
import jax.numpy as jnp
from jax import lax
from jax.experimental import pallas as pl
from jax.experimental.pallas import tpu as pltpu

F32, BF16, I32 = jnp.float32, jnp.bfloat16, jnp.int32

D = 1024
N_Q_HEADS, N_KV_HEADS, HEAD_DIM = 8, 2, 64
ATTN_BLOCK = 128
GLA_HEADS, GLA_K, GLA_V = 4, 64, 128
GLA_RANK = 16
GLA_NORMALIZER = 16.0
N_EXPERTS, FF = 16, 1024
CAPACITY_FACTOR = 2
EPS = 1e-6
NEG = -1e30
LOG2E = 1.4426950408889634

C_AQ, C_GV, C_GA, C_GG, C_GR, C_KV, C_GQ, C_GK, C_GL = 0, 512, 1024, 2048, 3072, 3584, 3840, 4096, 4352
NP = 4480

LANES = 128
ROW_ALIGN = 16
CHUNK = 64
RCHUNK = 96
ROUTE_TILE = 256
XS_W = D + LANES
VMEM_LIMIT = 56 * 1024 * 1024


def _cparams(sem):
    return pltpu.CompilerParams(dimension_semantics=sem, vmem_limit_bytes=VMEM_LIMIT)


def _dot(a, b):
    return jnp.dot(a, b, preferred_element_type=F32)


def _dot_nt(a, b):
    return lax.dot_general(a, b, (((1,), (1,)), ((), ())), preferred_element_type=F32)


def _dot_tn(a, b):
    return lax.dot_general(a, b, (((0,), (0,)), ((), ())), preferred_element_type=F32)


def _split2(x):
    hi = x.astype(BF16)
    lo = (x - hi.astype(F32)).astype(BF16)
    return hi, lo


def _split3(x):
    hi = x.astype(BF16)
    r = x - hi.astype(F32)
    mid = r.astype(BF16)
    lo = (r - mid.astype(F32)).astype(BF16)
    return hi, mid, lo


def _sigmoid(x):
    return 0.5 * jnp.tanh(0.5 * x) + 0.5


def _rms(x, w):
    return x * lax.rsqrt(jnp.mean(x * x, axis=-1, keepdims=True) + EPS) * w


def _ada_kernel(c_ref, w_ref, b_ref, o_ref):
    c = c_ref[...]
    s = c * jax.nn.sigmoid(c)
    s_hi, s_lo = _split2(s)
    w_hi, w_lo = _split2(w_ref[...])
    o_ref[...] = _dot(s_hi, w_hi) + _dot(s_lo, w_hi) + _dot(s_hi, w_lo) + b_ref[...]


def _ada(c, w_ada, b_ada):
    rows = c.shape[0]
    tn = 1024
    return pl.pallas_call(
        _ada_kernel,
        grid=(6 * D // tn,),
        in_specs=[pl.BlockSpec((rows, D), lambda j: (0, 0)),
                  pl.BlockSpec((D, tn), lambda j: (0, j)),
                  pl.BlockSpec((1, tn), lambda j: (0, j))],
        out_specs=pl.BlockSpec((rows, tn), lambda j: (0, j)),
        out_shape=jax.ShapeDtypeStruct((rows, 6 * D), F32),
        compiler_params=_cparams(("arbitrary",)),
        name="ada",
    )(c, w_ada, b_ada.reshape(1, 6 * D))


def _inproj_kernel(x_ref, ada_ref, nw_ref, w_ref, o_ref):
    x = x_ref[0]
    h = (_rms(x, nw_ref[...] * (1.0 + ada_ref[0, 1:2, :])) + ada_ref[0, 0:1, :]).astype(BF16)
    step = 1024
    for c0 in range(0, NP, step):
        c1 = min(c0 + step, NP)
        o_ref[0, :, c0:c1] = _dot(h, w_ref[:, c0:c1]).astype(BF16)


def _inproj(x, ada, norm_w, w_in_p):
    B, S, _ = x.shape
    tm = 1024
    return pl.pallas_call(
        _inproj_kernel,
        grid=(B, S // tm),
        in_specs=[pl.BlockSpec((1, tm, D), lambda b, i: (b, i, 0)),
                  pl.BlockSpec((1, 6, D), lambda b, i: (b, 0, 0)),
                  pl.BlockSpec((1, D), lambda b, i: (0, 0)),
                  pl.BlockSpec((D, NP), lambda b, i: (0, 0), pipeline_mode=pl.Buffered(1))],
        out_specs=pl.BlockSpec((1, tm, NP), lambda b, i: (b, i, 0)),
        out_shape=jax.ShapeDtypeStruct((B, S, NP), BF16),
        compiler_params=_cparams(("arbitrary", "arbitrary")),
        name="inproj",
    )(x, ada, norm_w.reshape(1, D), w_in_p)


ATTN_TQ = 2048
ATTN_QB = ATTN_TQ // ATTN_BLOCK


_ATTN_STACK = (0, 2, 1, 3)


def _attn_bias_table():
    L = ATTN_BLOCK
    group = N_Q_HEADS // N_KV_HEADS
    q = np.arange(L)[:, None]
    k = np.arange(3 * L)[None, :]
    dist = np.abs(q - k + L).astype(np.float32)
    slopes = np.asarray([2.0 ** (-8.0 * (h + 1) / N_Q_HEADS) for h in range(N_Q_HEADS)], np.float32)
    base = -slopes[:, None, None] * dist[None] * np.float32(LOG2E)
    inside = (dist <= L)[None]
    variants = []
    for ok_k in (np.ones_like(k, bool), k >= L, k < 2 * L):
        per_head = np.where(inside & ok_k[None], base, np.float32(NEG))
        variants.append(np.stack([np.concatenate([per_head[hkv * group + g] for g in _ATTN_STACK], axis=0)
                                  for hkv in range(N_KV_HEADS)]))
    return jnp.asarray(np.stack(variants), F32)


def _attn_kernel(sink_ref, q_ref, kvm_ref, kvp_ref, kvn_ref, bias_ref, o_ref):
    L = ATTN_BLOCK
    i = pl.program_id(1)
    last = pl.num_programs(1) - 1
    kv = jnp.concatenate([kvp_ref[0], kvm_ref[0], kvn_ref[0]], axis=0)
    k = kv[:, :LANES] * jnp.asarray(HEAD_DIM ** -0.5, BF16)
    v = kv[:, LANES:]
    lo = lax.broadcasted_iota(I32, (1, LANES), 1) < HEAD_DIM
    zero = jnp.zeros_like(k)

    def halves(t):
        sw = jnp.concatenate([t[:, HEAD_DIM:], t[:, :HEAD_DIM]], axis=1)
        return ((jnp.where(lo, t, zero), jnp.where(lo, zero, sw)),
                (jnp.where(lo, sw, zero), jnp.where(lo, zero, t)))

    kh, vh = halves(k), halves(v)
    group = N_Q_HEADS // N_KV_HEADS
    rowblk = lax.broadcasted_iota(I32, (group * L, 1), 0) // L
    for j in range(ATTN_QB):
        first = jnp.logical_and(i == 0, j == 0)
        final = jnp.logical_and(i == last, j == ATTN_QB - 1)
        var = jnp.where(first, 1, jnp.where(final, 2, 0))
        r0 = j * L
        for hkv in range(N_KV_HEADS):
            c0 = hkv * 2 * LANES
            qs = jnp.concatenate([q_ref[0, r0:r0 + L, c0:c0 + LANES],
                                  q_ref[0, r0:r0 + L, c0 + LANES:c0 + 2 * LANES]], axis=0)
            s = jnp.concatenate([_dot_nt(qs, kh[hkv][0][r0:r0 + 3 * L]),
                                 _dot_nt(qs, kh[hkv][1][r0:r0 + 3 * L])], axis=0) + bias_ref[var, hkv]
            snk = jnp.zeros((group * L, 1), F32)
            for b, g in enumerate(_ATTN_STACK):
                snk = jnp.where(rowblk == b, sink_ref[hkv * group + g], snk)
            mx = jnp.maximum(jnp.max(s, axis=-1, keepdims=True), snk)
            p = jnp.exp2(s - mx)
            rden = 1.0 / (jnp.sum(p, axis=-1, keepdims=True) + jnp.exp2(snk - mx))
            pb = p.astype(BF16)
            o = (_dot(pb[0:2 * L], vh[hkv][0][r0:r0 + 3 * L]) * rden[0:2 * L]
                 + _dot(pb[2 * L:4 * L], vh[hkv][1][r0:r0 + 3 * L]) * rden[2 * L:4 * L])
            o_ref[0, r0:r0 + L, c0:c0 + LANES] = o[0:L].astype(BF16)
            o_ref[0, r0:r0 + L, c0 + LANES:c0 + 2 * LANES] = o[L:2 * L].astype(BF16)


def _attention(proj, sink, bias):
    B, S, _ = proj.shape
    L = ATTN_BLOCK
    nb = S // L
    assert S % ATTN_TQ == 0 and nb >= 2
    kvc = C_KV // 256
    return pl.pallas_call(
        _attn_kernel,
        grid=(B, S // ATTN_TQ),
        in_specs=[pl.BlockSpec(memory_space=pltpu.SMEM),
                  pl.BlockSpec((1, ATTN_TQ, 512), lambda b, i: (b, i, C_AQ // 512)),
                  pl.BlockSpec((1, ATTN_TQ, 256), lambda b, i: (b, i, kvc)),
                  pl.BlockSpec((1, L, 256), lambda b, i: (b, jnp.maximum(i * ATTN_QB - 1, 0), kvc)),
                  pl.BlockSpec((1, L, 256), lambda b, i: (b, jnp.minimum(i * ATTN_QB + ATTN_QB, nb - 1), kvc)),
                  pl.BlockSpec((3, N_KV_HEADS, (N_Q_HEADS // N_KV_HEADS) * L, 3 * L), lambda b, i: (0, 0, 0, 0))],
        out_specs=pl.BlockSpec((1, ATTN_TQ, 512), lambda b, i: (b, i, 0)),
        out_shape=jax.ShapeDtypeStruct((B, S, 512), BF16),
        compiler_params=_cparams(("arbitrary", "arbitrary")),
        name="attn",
    )(sink * LOG2E, proj, proj, proj, proj, bias)


GLA_TS = 2048
GLA_C = 128
GLA_NC = GLA_TS // GLA_C


def _gla_tile(q_ref, k_ref, v_ref, gl_ref, w_ref, ba_ref, st_ref, rev):
    C = GLA_C
    x = _dot(gl_ref[0], w_ref[...]) + ba_ref[...]
    la = (jnp.minimum(x, 0.0) - jnp.log(1.0 + jnp.exp(-jnp.abs(x)))) * (1.0 / GLA_NORMALIZER)
    row = lax.broadcasted_iota(I32, (C, C), 0)
    col = lax.broadcasted_iota(I32, (C, C), 1)
    if rev:
        tri = jnp.where(col >= row, 1.0, 0.0).astype(BF16)
        keep = col > row
    else:
        tri = jnp.where(col <= row, 1.0, 0.0).astype(BF16)
        keep = col <= row
    hi, mid, lo3 = _split3(la)
    bcs, halves, ehs = [], [], []
    for c in range(GLA_NC):
        rs = slice(c * C, (c + 1) * C)
        bc = _dot(tri, hi[rs]) + _dot(tri, mid[rs]) + _dot(tri, lo3[rs])
        half = 0.5 * (bc[0:1] if rev else bc[C - 1:C])
        bcs.append(bc)
        halves.append(jnp.broadcast_to(half, bc.shape))
        ehs.append(jnp.exp(half))
    bc = jnp.concatenate(bcs, axis=0)
    half = jnp.concatenate(halves, axis=0)
    eh = jnp.concatenate([jnp.broadcast_to(e, (C, e.shape[1])) for e in ehs], axis=0)
    qe = q_ref[0].astype(F32) * (GLA_K ** -0.5) * jnp.exp(bc - half)
    ke = k_ref[0].astype(F32) * jnp.exp(half - bc)
    qt, kt = qe.astype(BF16), ke.astype(BF16)
    qh = (qe * eh).astype(BF16)
    kb = (ke * eh).astype(BF16)
    lo = lax.broadcasted_iota(I32, (1, LANES), 1) < GLA_K
    zero = jnp.zeros((GLA_TS, LANES), BF16)
    order = range(GLA_NC - 1, -1, -1) if rev else range(GLA_NC)
    outs = [[None] * GLA_HEADS for _ in range(GLA_NC)]
    for m in range(GLA_HEADS // 2):
        sl = slice(m * LANES, (m + 1) * LANES)
        ktm = (jnp.where(lo, kt[:, sl], zero), jnp.where(lo, zero, kt[:, sl]))
        qhm = (jnp.where(lo, qh[:, sl], zero), jnp.where(lo, zero, qh[:, sl]))
        st = st_ref[m]
        for c in order:
            rs = slice(c * C, (c + 1) * C)
            stb = st.astype(BF16)
            upd = []
            for par in range(2):
                h = 2 * m + par
                a = jnp.where(keep, _dot_nt(qt[rs, sl], ktm[par][rs]), 0.0).astype(BF16)
                vh = v_ref[0, rs, h * GLA_V:(h + 1) * GLA_V]
                outs[c][h] = _dot(a, vh) + _dot_nt(qhm[par][rs], stb)
                upd.append(_dot_tn(vh, kb[rs, sl]))
            dec = ehs[c][:, sl] * ehs[c][:, sl]
            st = st * dec + jnp.where(lo, upd[0], upd[1])
        st_ref[m] = st
    return outs


def _gla_fwd_kernel(q_ref, k_ref, v_ref, gl_ref, w_ref, ba_ref, o_ref, st_ref):
    @pl.when(pl.program_id(1) == 0)
    def _():
        st_ref[...] = jnp.zeros_like(st_ref)

    outs = _gla_tile(q_ref, k_ref, v_ref, gl_ref, w_ref, ba_ref, st_ref, rev=False)
    for c in range(GLA_NC):
        for h in range(GLA_HEADS):
            o_ref[0, c * GLA_C:(c + 1) * GLA_C, h * GLA_V:(h + 1) * GLA_V] = outs[c][h].astype(BF16)


def _gla_bwd_kernel(q_ref, k_ref, v_ref, gl_ref, w_ref, ba_ref, of_ref, gr_ref, nw_ref, o_ref, st_ref):
    @pl.when(pl.program_id(1) == 0)
    def _():
        st_ref[...] = jnp.zeros_like(st_ref)

    outs = _gla_tile(q_ref, k_ref, v_ref, gl_ref, w_ref, ba_ref, st_ref, rev=True)
    for h in range(GLA_HEADS):
        hs = slice(h * GLA_V, (h + 1) * GLA_V)
        o = jnp.concatenate([outs[c][h] for c in range(GLA_NC)], axis=0) + of_ref[0, :, hs].astype(F32)
        g = gr_ref[0, :, hs].astype(F32)
        o_ref[0, :, hs] = (_rms(o, nw_ref[...]) * (g * _sigmoid(g))).astype(BF16)


def _gla(proj, wa2_f, ba_f, wa2_b, ba_b, gla_norm):
    B, S, _ = proj.shape
    nt = S // GLA_TS
    assert S % GLA_TS == 0
    kw = GLA_HEADS * GLA_K
    wf = jnp.zeros((LANES, kw), F32).at[0:GLA_RANK].set(wa2_f).astype(BF16)
    wb = jnp.zeros((LANES, kw), F32).at[GLA_RANK:2 * GLA_RANK].set(wa2_b).astype(BF16)

    def specs(tile):
        return [pl.BlockSpec((1, GLA_TS, 256), lambda b, i: (b, tile(i), C_GQ // 256)),
                pl.BlockSpec((1, GLA_TS, 256), lambda b, i: (b, tile(i), C_GK // 256)),
                pl.BlockSpec((1, GLA_TS, 512), lambda b, i: (b, tile(i), C_GV // 512)),
                pl.BlockSpec((1, GLA_TS, LANES), lambda b, i: (b, tile(i), C_GL // LANES)),
                pl.BlockSpec((LANES, kw), lambda b, i: (0, 0)),
                pl.BlockSpec((1, kw), lambda b, i: (0, 0))]

    scratch = [pltpu.VMEM((GLA_HEADS // 2, GLA_V, LANES), F32)]
    fwd_tile = lambda i: i
    o_f = pl.pallas_call(
        _gla_fwd_kernel,
        grid=(B, nt),
        in_specs=specs(fwd_tile),
        out_specs=pl.BlockSpec((1, GLA_TS, 512), lambda b, i: (b, i, 0)),
        out_shape=jax.ShapeDtypeStruct((B, S, 512), BF16),
        scratch_shapes=scratch,
        compiler_params=_cparams(("arbitrary", "arbitrary")),
        name="gla_fwd",
    )(proj, proj, proj, proj, wf, ba_f.reshape(1, kw))
    bwd_tile = lambda i: nt - 1 - i
    return pl.pallas_call(
        _gla_bwd_kernel,
        grid=(B, nt),
        in_specs=specs(bwd_tile) + [
            pl.BlockSpec((1, GLA_TS, 512), lambda b, i: (b, bwd_tile(i), 0)),
            pl.BlockSpec((1, GLA_TS, 512), lambda b, i: (b, bwd_tile(i), C_GR // 512)),
            pl.BlockSpec((1, GLA_V), lambda b, i: (0, 0))],
        out_specs=pl.BlockSpec((1, GLA_TS, 512), lambda b, i: (b, bwd_tile(i), 0)),
        out_shape=jax.ShapeDtypeStruct((B, S, 512), BF16),
        scratch_shapes=scratch,
        compiler_params=_cparams(("arbitrary", "arbitrary")),
        name="gla_bwd",
    )(proj, proj, proj, proj, wb, ba_b.reshape(1, kw), o_f, proj, gla_norm.reshape(1, GLA_V))


POST_SUB = 256

def _post_kernel(x_ref, attn_ref, og_ref, ga_ref, gg_ref, ada_ref, wba_ref, wbg_ref, wo_ref, npost_ref,
                 npre_ref, wr_ref, x1_ref, h2_ref, g3_ref, afft_ref):
    w_post = ada_ref[0, 2:3, :] * npost_ref[...]
    w_pre = (1.0 + ada_ref[0, 4:5, :]) * npre_ref[...]
    shift = ada_ref[0, 3:4, :]
    for r0 in range(0, x_ref.shape[1], POST_SUB):
        rs = slice(r0, r0 + POST_SUB)
        ga = ga_ref[0, rs, :].astype(F32)
        gg = gg_ref[0, rs, :].astype(F32)
        merged = ((jnp.tanh(ga) + 1.0) * _dot(attn_ref[0, rs, :], wba_ref[...])
                  + (jnp.tanh(gg) + 1.0) * _dot(og_ref[0, rs, :], wbg_ref[...]))
        mix = _dot(merged.astype(BF16), wo_ref[...])
        x1 = x_ref[0, rs, :] + _rms(mix, w_post)
        x1_ref[0, rs, :] = x1
        h2 = _rms(x1, w_pre) + shift
        h2_ref[0, rs, :] = h2.astype(BF16)
        r = _dot(h2.astype(BF16), wr_ref[...])
        logits = r[:, 0:LANES] + r[:, LANES:2 * LANES]
        lane = lax.broadcasted_iota(I32, logits.shape, 1)
        logits = jnp.where(lane < N_EXPERTS, logits, NEG)
        e = jnp.exp(logits - jnp.max(logits, axis=-1, keepdims=True))
        aff = e / jnp.sum(e, axis=-1, keepdims=True)
        a_hi, a_mid, a_lo = _split3(aff)
        g3 = (a_hi.astype(F32) + pltpu.roll(a_mid.astype(F32), N_EXPERTS, 1)
              + pltpu.roll(a_lo.astype(F32), 2 * N_EXPERTS, 1))
        g3_ref[0, rs, :] = g3.astype(BF16)
        afft_ref[:, rs] = aff.T[0:N_EXPERTS, :]


def _post(x, attn, o_g, proj, ada, w_ba, w_bg, w_out, n_post, n_pre, wr):
    B, S, _ = x.shape
    tm = 1024
    nt = S // tm
    const = lambda b, i: (0, 0)
    return pl.pallas_call(
        _post_kernel,
        grid=(B, nt),
        in_specs=[pl.BlockSpec((1, tm, D), lambda b, i: (b, i, 0)),
                  pl.BlockSpec((1, tm, 512), lambda b, i: (b, i, 0)),
                  pl.BlockSpec((1, tm, 512), lambda b, i: (b, i, 0)),
                  pl.BlockSpec((1, tm, D), lambda b, i: (b, i, C_GA // D)),
                  pl.BlockSpec((1, tm, D), lambda b, i: (b, i, C_GG // D)),
                  pl.BlockSpec((1, 6, D), lambda b, i: (b, 0, 0)),
                  pl.BlockSpec((512, D), const),
                  pl.BlockSpec((512, D), const),
                  pl.BlockSpec((D, D), const),
                  pl.BlockSpec((1, D), const),
                  pl.BlockSpec((1, D), const),
                  pl.BlockSpec((D, 2 * LANES), const)],
        out_specs=[pl.BlockSpec((1, tm, D), lambda b, i: (b, i, 0)),
                   pl.BlockSpec((1, tm, D), lambda b, i: (b, i, 0)),
                   pl.BlockSpec((1, tm, LANES), lambda b, i: (b, i, 0)),
                   pl.BlockSpec((N_EXPERTS, tm), lambda b, i: (0, b * nt + i))],
        out_shape=[jax.ShapeDtypeStruct((B, S, D), F32),
                   jax.ShapeDtypeStruct((B, S, D), BF16),
                   jax.ShapeDtypeStruct((B, S, LANES), BF16),
                   jax.ShapeDtypeStruct((N_EXPERTS, B * S), F32)],
        compiler_params=_cparams(("arbitrary", "arbitrary")),
        name="post",
    )(x, attn, o_g, proj, proj, ada, w_ba, w_bg, w_out, n_post.reshape(1, D), n_pre.reshape(1, D), wr)


def _select_kernel(aff_ref, selpos_ref, post_ref, bs_ref, *, cap, n_tok):
    nblk = n_tok // LANES

    def count(mask):
        return jnp.sum(jnp.where(mask, 1.0, 0.0), axis=1, keepdims=True).astype(I32)

    def bit_body(it, bits):
        cand = bits | jnp.left_shift(jnp.int32(1), 30 - it)
        return jnp.where(count(aff_ref[...] >= pltpu.bitcast(cand, F32)) >= cap, cand, bits)

    thr = pltpu.bitcast(lax.fori_loop(0, 31, bit_body, jnp.zeros((N_EXPERTS, 1), I32)), F32)
    need = cap - count(aff_ref[...] > thr)
    r = lax.broadcasted_iota(I32, (LANES, LANES), 0)
    c = lax.broadcasted_iota(I32, (LANES, LANES), 1)
    scan_mat = jnp.concatenate([jnp.where(r <= c, 1.0, 0.0), jnp.ones((LANES, LANES), F32)], axis=1).astype(BF16)
    blk_lane = lax.broadcasted_iota(I32, bs_ref.shape, 1)
    filler = jnp.full((LANES - N_EXPERTS, LANES), -1.0, F32)
    reps = bs_ref.shape[1] // LANES

    def taken_before(c_gt, c_eq):
        return jnp.concatenate([c_gt + jnp.minimum(c_eq, need)] * reps, axis=1)

    def blk_body(j, carry):
        c_gt, c_eq, bs = carry
        off = pl.multiple_of(j * LANES, LANES)
        a = aff_ref[:, pl.ds(off, LANES)]
        gt = a > thr
        eq = a == thr
        both = jnp.concatenate([jnp.where(gt, 1.0, 0.0), jnp.where(eq, 1.0, 0.0)], axis=0).astype(BF16)
        scan = _dot(both, scan_mat).astype(I32)
        ex_gt = c_gt + scan[0:N_EXPERTS, 0:LANES] - jnp.where(gt, 1, 0)
        ex_eq = c_eq + scan[N_EXPERTS:, 0:LANES] - jnp.where(eq, 1, 0)
        sel = jnp.logical_or(gt, jnp.logical_and(eq, ex_eq < need))
        pos = ex_gt + jnp.minimum(ex_eq, need)
        sp = jnp.where(sel, pos, -1)
        selpos_ref[:, pl.ds(off, LANES)] = sp
        post_ref[pl.ds(off, LANES), :] = jnp.concatenate([sp.astype(F32), filler], axis=0).T
        bs = jnp.where(blk_lane == j, taken_before(c_gt, c_eq), bs)
        return c_gt + scan[0:N_EXPERTS, LANES:], c_eq + scan[N_EXPERTS:, LANES:], bs

    z = jnp.zeros((N_EXPERTS, LANES), I32)
    c_gt, c_eq, bs = lax.fori_loop(0, nblk, blk_body, (z, z, jnp.zeros(bs_ref.shape, I32)), unroll=4)
    bs_ref[...] = jnp.where(blk_lane >= nblk, taken_before(c_gt, c_eq), bs)


def _select(aff_t, cap):
    n_tok = aff_t.shape[1]
    nbp = ((n_tok // LANES + 1 + LANES - 1) // LANES) * LANES
    return pl.pallas_call(
        functools.partial(_select_kernel, cap=cap, n_tok=n_tok),
        out_shape=[jax.ShapeDtypeStruct((N_EXPERTS, n_tok), I32),
                   jax.ShapeDtypeStruct((n_tok, LANES), F32),
                   jax.ShapeDtypeStruct((N_EXPERTS, nbp), I32)],
        compiler_params=pltpu.CompilerParams(vmem_limit_bytes=VMEM_LIMIT),
        name="select",
    )(aff_t)


def _row_window(bs_ref, e, tile):
    per = ROUTE_TILE // LANES
    start = bs_ref[e, tile * per]
    cnt = bs_ref[e, tile * per + per] - start
    base = (start // ROW_ALIGN) * ROW_ALIGN
    return start, cnt, base, start - base


def _compact_kernel(bs_ref, h2_ref, g3_ref, selpos_ref, xs_ref, stage_ref, over_ref, carry_ref, sems, osems, pend_ref):
    t = pl.program_id(0)
    nt = pl.num_programs(0)
    par = t % 2
    cap = xs_ref.shape[1] - CHUNK

    def strip_copy(slot, e, base):
        return pltpu.make_async_copy(stage_ref.at[slot, pl.ds(e * CHUNK, CHUNK), :],
                                     xs_ref.at[e, pl.ds(pl.multiple_of(base, ROW_ALIGN), CHUNK), :], sems.at[slot])

    def over_copy(e, base):
        return pltpu.make_async_copy(over_ref.at[e], xs_ref.at[e, pl.ds(pl.multiple_of(base, ROW_ALIGN), CHUNK), :],
                                     osems.at[e])

    @pl.when(t == 0)
    def _():
        over_ref[0] = jnp.zeros((CHUNK, XS_W), BF16)
        carry_ref[...] = jnp.zeros_like(carry_ref)
        for e in range(N_EXPERTS):
            pend_ref[e] = 0
            pltpu.make_async_copy(over_ref.at[0], xs_ref.at[e, pl.ds(cap, CHUNK), :], osems.at[e]).start()
        for e in range(N_EXPERTS):
            pltpu.make_async_copy(over_ref.at[0], xs_ref.at[e, pl.ds(cap, CHUNK), :], osems.at[e]).wait()

    jrow = lax.broadcasted_iota(I32, (CHUNK, ROUTE_TILE), 0)
    pieces = []
    for e in range(N_EXPERTS):
        _, _, base, _ = _row_window(bs_ref, e, t)
        pieces.append(jnp.where(selpos_ref[e:e + 1, :] - base == jrow, 1.0, 0.0).astype(BF16))
    onehot = jnp.concatenate(pieces, axis=0)
    stage_ref[par, :, 0:D] = _dot(onehot, h2_ref[...]).astype(BF16)
    stage_ref[par, :, D:XS_W] = _dot(onehot, g3_ref[...]).astype(BF16)

    @pl.when(t > 0)
    def _():
        for e in range(N_EXPERTS):
            strip_copy(1 - par, e, 0).wait()

    r16 = lax.broadcasted_iota(I32, (ROW_ALIGN, XS_W), 0)
    for e in range(N_EXPERTS):
        start, cnt, base, off = _row_window(bs_ref, e, t)
        head = pl.ds(e * CHUNK, ROW_ALIGN)
        stage_ref[par, head, :] = jnp.where(r16 < off, carry_ref[e], stage_ref[par, head, :])

        @pl.when(pend_ref[e] == 1)
        def _():
            over_copy(e, 0).wait()
            pend_ref[e] = 0

        strip_copy(par, e, base).start()
        end = start + cnt
        nch = (off + cnt + CHUNK - 1) // CHUNK
        gbase = (end // ROW_ALIGN) * ROW_ALIGN
        src = pl.multiple_of(e * CHUNK + jnp.minimum(gbase - base, CHUNK - ROW_ALIGN), ROW_ALIGN)
        carry_ref[e] = stage_ref[par, pl.ds(src, ROW_ALIGN), :]

        def more(cidx, _):
            cb = base + cidx * CHUNK

            @pl.when(pend_ref[e] == 1)
            def _():
                over_copy(e, 0).wait()

            oh = jnp.where(selpos_ref[e:e + 1, :] - cb == jrow, 1.0, 0.0).astype(BF16)
            over_ref[e, :, 0:D] = _dot(oh, h2_ref[...]).astype(BF16)
            over_ref[e, :, D:XS_W] = _dot(oh, g3_ref[...]).astype(BF16)
            over_copy(e, cb).start()
            pend_ref[e] = 1

            @pl.when(jnp.logical_and(cidx == nch - 1, gbase - cb < CHUNK))
            def _():
                carry_ref[e] = over_ref[e, pl.ds(pl.multiple_of(gbase - cb, ROW_ALIGN), ROW_ALIGN), :]
            return 0

        lax.fori_loop(1, nch, more, 0)

    @pl.when(t == nt - 1)
    def _():
        for e in range(N_EXPERTS):
            strip_copy(par, e, 0).wait()

            @pl.when(pend_ref[e] == 1)
            def _():
                over_copy(e, 0).wait()
                pend_ref[e] = 0


def _compact(bs, h2, g3, selpos, cap):
    n_tok = h2.shape[0]
    nt = n_tok // ROUTE_TILE
    grid_spec = pltpu.PrefetchScalarGridSpec(
        num_scalar_prefetch=1,
        grid=(nt,),
        in_specs=[pl.BlockSpec((ROUTE_TILE, D), lambda t, bs: (t, 0)),
                  pl.BlockSpec((ROUTE_TILE, LANES), lambda t, bs: (t, 0)),
                  pl.BlockSpec((N_EXPERTS, ROUTE_TILE), lambda t, bs: (0, t))],
        out_specs=pl.BlockSpec(memory_space=pl.ANY),
        scratch_shapes=[pltpu.VMEM((2, N_EXPERTS * CHUNK, XS_W), BF16),
                        pltpu.VMEM((N_EXPERTS, CHUNK, XS_W), BF16),
                        pltpu.VMEM((N_EXPERTS, ROW_ALIGN, XS_W), BF16),
                        pltpu.SemaphoreType.DMA((2,)),
                        pltpu.SemaphoreType.DMA((N_EXPERTS,)),
                        pltpu.SMEM((N_EXPERTS,), I32)])
    return pl.pallas_call(
        _compact_kernel,
        grid_spec=grid_spec,
        out_shape=jax.ShapeDtypeStruct((N_EXPERTS, cap + CHUNK, XS_W), BF16),
        compiler_params=_cparams(("arbitrary",)),
        name="compact",
    )(bs, h2, g3, selpos)


def _ffn_kernel(xs_ref, wg_ref, wu_ref, wd_ref, ys_ref, wgb_ref, wub_ref, wdb_ref):
    g = pl.program_id(0)
    j = pl.program_id(1)
    slab = wg_ref.shape[1]

    @pl.when(g < N_EXPERTS)
    def _():
        rows = pl.ds(pl.multiple_of(j * slab, slab), slab)
        wgb_ref[g % 2, rows, :] = wg_ref[0].astype(BF16)
        wub_ref[g % 2, rows, :] = wu_ref[0].astype(BF16)
        wdb_ref[g % 2, rows, :] = wd_ref[0].astype(BF16)

    @pl.when(g > 0)
    def _():
        e = g - 1
        cur = e % 2
        xs = xs_ref[0]
        x = xs[:, 0:D]
        lane = lax.broadcasted_iota(I32, (1, LANES), 1)
        mine = jnp.logical_or(lane == e, jnp.logical_or(lane == e + N_EXPERTS, lane == e + 2 * N_EXPERTS))
        gate = jnp.sum(jnp.where(mine, xs[:, D:XS_W].astype(F32), 0.0), axis=1, keepdims=True)
        a = _dot(x, wgb_ref[cur])
        hid = (a * _sigmoid(a) * _dot(x, wub_ref[cur])).astype(BF16)
        ys_ref[0] = (_dot(hid, wdb_ref[cur]) * gate).astype(BF16)


def _ffn(xs, w_gate, w_up, w_down, cap):
    tm = min(1024, cap)
    nj = cap // tm
    assert cap % tm == 0 and D % nj == 0 and FF % nj == 0
    last = N_EXPERTS - 1
    wmap = lambda g, j: (jnp.minimum(g, last), jnp.where(g <= last, j, nj - 1), 0)
    xmap = lambda g, j: (jnp.maximum(g - 1, 0), jnp.where(g == 0, 0, j), 0)
    return pl.pallas_call(
        _ffn_kernel,
        grid=(N_EXPERTS + 1, nj),
        in_specs=[pl.BlockSpec((1, tm, XS_W), xmap),
                  pl.BlockSpec((1, D // nj, FF), wmap),
                  pl.BlockSpec((1, D // nj, FF), wmap),
                  pl.BlockSpec((1, FF // nj, D), wmap)],
        out_specs=pl.BlockSpec((1, tm, D), xmap),
        out_shape=jax.ShapeDtypeStruct((N_EXPERTS, cap, D), BF16),
        scratch_shapes=[pltpu.VMEM((2, D, FF), BF16), pltpu.VMEM((2, D, FF), BF16), pltpu.VMEM((2, FF, D), BF16)],
        compiler_params=_cparams(("arbitrary", "arbitrary")),
        name="ffn",
    )(xs, w_gate, w_up, w_down)


def _combine_kernel(bs_ref, x1_ref, post_ref, ada_ref, nw_ref, expand_ref, ys_ref, o_ref,
                    strips_ref, over_ref, acc_ref, sems, osems, *, cap):
    nts = pl.num_programs(1)
    t = pl.program_id(0) * nts + pl.program_id(1)
    nt = pl.num_programs(0) * nts

    def strip_base(e, tile):
        _, _, base, _ = _row_window(bs_ref, e, tile)
        return jnp.minimum(base, cap - RCHUNK)

    def strip_copy(e, tile, slot):
        src = ys_ref.at[e, pl.ds(pl.multiple_of(strip_base(e, tile), ROW_ALIGN), RCHUNK), :]
        return pltpu.make_async_copy(src, strips_ref.at[slot, pl.ds(e * RCHUNK, RCHUNK), :], sems.at[slot])

    def over_copy(e, cb):
        return pltpu.make_async_copy(ys_ref.at[e, pl.ds(pl.multiple_of(cb, ROW_ALIGN), RCHUNK), :], over_ref.at[e],
                                     osems.at[e])

    def n_strips(e):
        _, cnt, _, off = _row_window(bs_ref, e, t)
        return (off + cnt + RCHUNK - 1) // RCHUNK

    def over_base(e, cidx):
        _, _, base, _ = _row_window(bs_ref, e, t)
        return jnp.minimum(base + cidx * RCHUNK, cap - RCHUNK)

    @pl.when(t == 0)
    def _():
        for e in range(N_EXPERTS):
            strip_copy(e, 0, 0).start()

    @pl.when(t + 1 < nt)
    def _():
        for e in range(N_EXPERTS):
            strip_copy(e, t + 1, (t + 1) % 2).start()

    for e in range(N_EXPERTS):
        @pl.when(n_strips(e) >= 2)
        def _():
            over_copy(e, over_base(e, 1)).start()

    pos = post_ref[...]
    elane = lax.broadcasted_iota(I32, (1, LANES), 1)
    bases = jnp.zeros((1, LANES), I32)
    for e in range(N_EXPERTS):
        bases = jnp.where(elane == e, strip_base(e, t), bases)
    rel = pos - bases.astype(F32)
    inside = jnp.logical_and(pos >= 0.0, jnp.logical_and(rel >= 0.0, rel < float(RCHUNK)))
    rel_x = _dot(jnp.where(inside, rel, -1.0).astype(BF16), expand_ref[...])
    want = (lax.broadcasted_iota(I32, (1, N_EXPERTS * RCHUNK), 1) % RCHUNK).astype(F32)
    onehot = jnp.where(rel_x == want, 1.0, 0.0).astype(BF16)

    slot = t % 2
    for e in range(N_EXPERTS):
        strip_copy(e, t, slot).wait()
    acc_ref[...] = _dot(onehot, strips_ref[slot])

    lane_c = lax.broadcasted_iota(I32, (1, RCHUNK), 1)
    for e in range(N_EXPERTS):
        _, _, base, _ = _row_window(bs_ref, e, t)
        nch = n_strips(e)

        def more(cidx, _):
            cb = over_base(e, cidx)

            @pl.when(cidx >= 2)
            def _():
                over_copy(e, cb).start()

            tgt = (lane_c + cb).astype(F32)
            prev_end = (jnp.minimum(base + (cidx - 1) * RCHUNK, cap - RCHUNK) + RCHUNK).astype(F32)
            col = post_ref[:, e:e + 1]
            oh = jnp.where(jnp.logical_and(col == tgt, col >= prev_end), 1.0, 0.0).astype(BF16)
            over_copy(e, cb).wait()
            acc_ref[...] += _dot(oh, over_ref[e])
            return 0

        lax.fori_loop(1, nch, more, 0)

    y = acc_ref[...]
    o_ref[0] = x1_ref[0] + _rms(y, ada_ref[0, 5:6, :] * nw_ref[...])


def _combine(bs, x1, post, ada, norm_w, expand, ys, cap):
    B, S, _ = x1.shape
    nts = S // ROUTE_TILE
    grid_spec = pltpu.PrefetchScalarGridSpec(
        num_scalar_prefetch=1,
        grid=(B, nts),
        in_specs=[pl.BlockSpec((1, ROUTE_TILE, D), lambda b, i, bs: (b, i, 0)),
                  pl.BlockSpec((ROUTE_TILE, LANES), lambda b, i, bs: (b * nts + i, 0)),
                  pl.BlockSpec((1, 6, D), lambda b, i, bs: (b, 0, 0)),
                  pl.BlockSpec((1, D), lambda b, i, bs: (0, 0)),
                  pl.BlockSpec((LANES, N_EXPERTS * RCHUNK), lambda b, i, bs: (0, 0)),
                  pl.BlockSpec(memory_space=pl.ANY)],
        out_specs=pl.BlockSpec((1, ROUTE_TILE, D), lambda b, i, bs: (b, i, 0)),
        scratch_shapes=[pltpu.VMEM((2, N_EXPERTS * RCHUNK, D), BF16),
                        pltpu.VMEM((N_EXPERTS, RCHUNK, D), BF16),
                        pltpu.VMEM((ROUTE_TILE, D), F32),
                        pltpu.SemaphoreType.DMA((2,)),
                        pltpu.SemaphoreType.DMA((N_EXPERTS,))])
    return pl.pallas_call(
        functools.partial(_combine_kernel, cap=cap),
        grid_spec=grid_spec,
        out_shape=jax.ShapeDtypeStruct((B, S, D), F32),
        compiler_params=_cparams(("arbitrary", "arbitrary")),
        name="combine",
    )(bs, x1, post, ada, norm_w.reshape(1, D), expand, ys)


def _prepare_weights(w_in, w_branch_attn, w_branch_gla, w_out, w_router, w_gate_e, w_up_e, w_down_e):
    aq, ak, av, gq, gk, gv, gr, glf, glb, ga, gg = jnp.split(
        w_in, np.cumsum([512, 128, 128, 256, 256, 512, 512, 16, 16, 1024]).tolist(), axis=1)
    pad = jnp.zeros((D, NP - C_GL - 2 * GLA_RANK), F32)
    w_in_p = jnp.concatenate([aq * LOG2E, gv, 0.5 * ga, 0.5 * gg, gr, ak, av, gq, gk, glf, glb, pad],
                             axis=1).astype(BF16)
    wr = jnp.zeros((D, LANES), F32).at[:, 0:N_EXPERTS].set(w_router)
    wr_hi = wr.astype(BF16)
    wr_cat = jnp.concatenate([wr_hi, (wr - wr_hi.astype(F32)).astype(BF16)], axis=1)
    expand = np.zeros((LANES, N_EXPERTS * RCHUNK), np.float32)
    for e in range(N_EXPERTS):
        expand[e, e * RCHUNK:(e + 1) * RCHUNK] = 1.0
    return dict(w_in_p=w_in_p, w_ba=w_branch_attn.astype(BF16), w_bg=w_branch_gla.astype(BF16),
                w_out=(0.5 * w_out).astype(BF16), wr=wr_cat, w_gate=w_gate_e,
                w_up=w_up_e, w_down=w_down_e, expand=jnp.asarray(expand, BF16),
                bias=_attn_bias_table())


def _layer(x, ada, w, p):
    B, S, _ = x.shape
    n_tok = B * S
    cap = CAPACITY_FACTOR * n_tok // N_EXPERTS
    assert n_tok % ROUTE_TILE == 0 and S % ROUTE_TILE == 0 and cap % ROW_ALIGN == 0 and cap >= RCHUNK
    proj = _inproj(x, ada, p['norm_pre_mix'], w['w_in_p'])
    attn = _attention(proj, p['attn_sink'], w['bias'])
    o_g = _gla(proj, p['gla_wa2_fwd'], p['gla_ba_fwd'], p['gla_wa2_bwd'], p['gla_ba_bwd'], p['gla_norm'])
    x1, h2, g3, aff_t = _post(x, attn, o_g, proj, ada, w['w_ba'], w['w_bg'], w['w_out'],
                              p['norm_post_mix'], p['norm_pre_ffn'], w['wr'])
    selpos, post, bs = _select(aff_t, cap)
    xs = _compact(bs, h2.reshape(n_tok, D), g3.reshape(n_tok, LANES), selpos, cap)
    ys = _ffn(xs, w['w_gate'], w['w_up'], w['w_down'], cap)
    return _combine(bs, x1, post, ada, p['norm_post_ffn'], w['expand'], ys, cap)


def kernel(x_prompt, x_sample, c_prompt, c_sample, w_ada, b_ada, norm_pre_mix, norm_post_mix, w_in, attn_sink, gla_wa2_fwd, gla_ba_fwd, gla_wa2_bwd, gla_ba_bwd, gla_norm, w_branch_attn, w_branch_gla, w_out, norm_pre_ffn, norm_post_ffn, w_router, w_gate_e, w_up_e, w_down_e):
    assert w_ada.shape[0] == 1, "one layer"
    p = dict(norm_pre_mix=norm_pre_mix[0], norm_post_mix=norm_post_mix[0], attn_sink=attn_sink[0],
             gla_wa2_fwd=gla_wa2_fwd[0], gla_ba_fwd=gla_ba_fwd[0], gla_wa2_bwd=gla_wa2_bwd[0],
             gla_ba_bwd=gla_ba_bwd[0], gla_norm=gla_norm[0], norm_pre_ffn=norm_pre_ffn[0],
             norm_post_ffn=norm_post_ffn[0])
    w = _prepare_weights(w_in[0], w_branch_attn[0], w_branch_gla[0], w_out[0], w_router[0],
                         w_gate_e[0], w_up_e[0], w_down_e[0])
    bp, bs_ = c_prompt.shape[0], c_sample.shape[0]
    rows = -(-(bp + bs_) // 8) * 8
    c_all = jnp.concatenate([c_prompt, c_sample, jnp.zeros((rows - bp - bs_, D), F32)], axis=0)
    ada = _ada(c_all, w_ada[0], b_ada[0]).reshape(rows, 6, D)
    y_prompt = _layer(x_prompt, ada[0:bp], w, p)
    y_sample = _layer(x_sample, ada[bp:bp + bs_], w, p)
    return (y_prompt, y_sample)
```
